```python
import jax, jax.numpy as jnp
from jax import lax
import numpy as np

D_MODEL = 1024
BATCH = 4
SEQ = 8192
DEPTH = 4

HG_HEADS = 8
HG_DK = 128
HG_DV = D_MODEL // HG_HEADS
HG_KEY_W = HG_HEADS * HG_DK
HG_VAL_W = HG_HEADS * HG_DV
HG_CHUNK = 64
FOX_HEADS = 8
FOX_DH = D_MODEL // FOX_HEADS
FOX_W = FOX_HEADS * FOX_DH
ATTN_BLOCK = 128
N_GROUPS = 4
EXPERTS_PER_GROUP = 8
N_EXPERTS = N_GROUPS * EXPERTS_PER_GROUP
TOP_K = 2
D_EXPERT = D_MODEL // 2
MOE_BLOCK = 128
IN_SPLITS = (HG_KEY_W, HG_KEY_W, HG_VAL_W, HG_VAL_W, FOX_W, FOX_W, FOX_W, FOX_HEADS, D_MODEL, D_MODEL)
IN_COLS = sum(IN_SPLITS)
ALPHA = (2 * DEPTH) ** 0.25
BETA = (8 * DEPTH) ** -0.25
LN_EPS = 1e-5
NORM_EPS = 1e-6

kernel_name = 'hgrn2_fox_gated_hier_moe_deepnorm'


def layer_norm(x, g, b):
    xf = x.astype(jnp.float32)
    mu = jnp.mean(xf, -1, keepdims=True)
    xc = xf - mu
    var = jnp.mean(xc * xc, -1, keepdims=True)
    return (xc * lax.rsqrt(var + LN_EPS) * g + b).astype(x.dtype)


def rms_norm(x, g):
    xf = x.astype(jnp.float32)
    return xf * lax.rsqrt(jnp.mean(xf * xf, -1, keepdims=True) + NORM_EPS) * g


def to_heads(a, n_heads):
    b, s, _ = a.shape
    return a.reshape(b, s, n_heads, -1).transpose(0, 2, 1, 3)


def from_heads(a):
    b, h, s, d = a.shape
    return a.transpose(0, 2, 1, 3).reshape(b, s, h * d)


def hgrn2_chunkwise(q, k, v, log_f):
    b, h, s, dk = q.shape
    dv = v.shape[-1]
    n = s // HG_CHUNK

    def to_chunks(a):
        return a.reshape(b, h, n, HG_CHUNK, a.shape[-1]).transpose(2, 0, 1, 3, 4)

    causal = jnp.tril(jnp.ones((HG_CHUNK, HG_CHUNK), bool))

    def step(state, inp):
        q_c, k_c, v_c, g_c = inp
        cum = jnp.cumsum(g_c, axis=2)
        diff = cum[:, :, :, None, :] - cum[:, :, None, :, :]
        decay = jnp.exp(jnp.where(causal[:, :, None], diff, -jnp.inf))
        scores = jnp.einsum('bhtk,bhsk,bhtsk->bhts', q_c, k_c, decay)
        o_intra = jnp.einsum('bhts,bhsv->bhtv', scores, v_c)
        o_inter = jnp.einsum('bhtk,bhkv->bhtv', q_c * jnp.exp(cum), state)
        last = cum[:, :, -1:, :]
        k_dec = k_c * jnp.exp(last - cum)
        new_state = jnp.exp(last[:, :, 0, :])[..., None] * state + jnp.einsum('bhsk,bhsv->bhkv', k_dec, v_c)
        return new_state, o_intra + o_inter

    state0 = jnp.zeros((b, h, dk, dv), jnp.float32)
    _, o = lax.scan(step, state0, (to_chunks(q), to_chunks(k), to_chunks(v), to_chunks(log_f)))
    return o.transpose(1, 2, 0, 3, 4).reshape(b, h, s, dv)


def forgetting_attention(q, k, v, log_f):
    s = q.shape[2]
    cum = jnp.cumsum(log_f, axis=-1)
    scale = FOX_DH ** -0.5
    outs = []
    for i in range(s // ATTN_BLOCK):
        lo, hi = i * ATTN_BLOCK, (i + 1) * ATTN_BLOCK
        logits = jnp.einsum('bhtd,bhsd->bhts', q[:, :, lo:hi], k[:, :, :hi]) * scale
        logits = logits + cum[:, :, lo:hi, None] - cum[:, :, None, :hi]
        mask = jnp.arange(hi)[None, :] <= (lo + jnp.arange(ATTN_BLOCK))[:, None]
        p = jax.nn.softmax(jnp.where(mask, logits, -jnp.inf), axis=-1)
        outs.append(jnp.einsum('bhts,bhsd->bhtd', p, v[:, :, :hi]))
    return jnp.concatenate(outs, axis=2)


def hybrid_mixer(x, w_in, b_fox_f, fox_q_gain, fox_k_gain, hg_lb, hg_norm_gain, w_br_a, w_br_b, w_out):
    f32 = jnp.float32
    proj = x @ w_in
    offsets = [int(o) for o in np.cumsum(IN_SPLITS)[:-1]]
    hq, hf, hi, hg, fq, fk, fv, ff, gate_a, gate_b = jnp.split(proj, offsets, axis=-1)
    q_a = jax.nn.silu(to_heads(hq, HG_HEADS).astype(f32))
    lb = hg_lb.reshape(HG_HEADS, 1, HG_DK)
    forget = lb + (1.0 - lb) * jax.nn.sigmoid(to_heads(hf, HG_HEADS).astype(f32))
    o_a = hgrn2_chunkwise(q_a, 1.0 - forget, to_heads(hi, HG_HEADS).astype(f32), jnp.log(forget))
    o_a = from_heads(rms_norm(o_a, hg_norm_gain)) * jax.nn.silu(hg.astype(f32))
    y_a = o_a.astype(x.dtype) @ w_br_a
    q_b = rms_norm(to_heads(fq, FOX_HEADS), fox_q_gain)
    k_b = rms_norm(to_heads(fk, FOX_HEADS), fox_k_gain)
    v_b = to_heads(fv, FOX_HEADS).astype(f32)
    log_f = jax.nn.log_sigmoid((ff + b_fox_f).astype(f32)).transpose(0, 2, 1)
    o_b = from_heads(forgetting_attention(q_b, k_b, v_b, log_f))
    y_b = o_b.astype(x.dtype) @ w_br_b
    merged = jax.nn.sigmoid(gate_a) * y_a + jax.nn.sigmoid(gate_b) * y_b
    return merged @ w_out


def grouped_expert_ffn(xt, expert_idx, w_gate, w_up, w_down):
    t, d = xt.shape
    a = t * TOP_K
    flat_e = expert_idx.reshape(a)
    flat_tok = jnp.arange(a, dtype=jnp.int32) // TOP_K
    order = jnp.argsort(flat_e)
    sorted_e = flat_e[order]
    counts = jnp.bincount(flat_e, length=N_EXPERTS)
    padded = (counts + MOE_BLOCK - 1) // MOE_BLOCK * MOE_BLOCK
    pad_end = jnp.cumsum(padded)
    pad_start = pad_end - padded
    start = jnp.cumsum(counts) - counts
    dest = pad_start[sorted_e] + jnp.arange(a, dtype=jnp.int32) - start[sorted_e]
    n_blocks = -(-a // MOE_BLOCK) + N_EXPERTS
    n_slots = n_blocks * MOE_BLOCK
    slot_tok = jnp.full((n_slots,), t, jnp.int32).at[dest].set(flat_tok[order])
    block_start = jnp.arange(n_blocks, dtype=pad_end.dtype) * MOE_BLOCK
    block_expert = jnp.minimum(jnp.searchsorted(pad_end, block_start, side='right'), N_EXPERTS - 1)
    x_pad = jnp.concatenate([xt, jnp.zeros((1, d), xt.dtype)], axis=0)
    xb = x_pad[slot_tok].reshape(n_blocks, MOE_BLOCK, d)

    def expert_block(args):
        xblk, e = args
        h = jax.nn.silu(xblk @ w_gate[e]) * (xblk @ w_up[e])
        return h @ w_down[e]

    yb = lax.map(expert_block, (xb, block_expert)).reshape(n_slots, d)
    dest_by_assign = jnp.zeros((a,), dest.dtype).at[order].set(dest)
    return yb[dest_by_assign].reshape(t, TOP_K, d)


def hier_moe(x, rg_w, rg_b, re_w, re_b, w_gate, w_up, w_down):
    f32 = jnp.float32
    b, s, d = x.shape
    t = b * s
    xt = x.reshape(t, d)
    g_logits = (xt @ rg_w).astype(f32) + rg_b
    grp = jnp.argmax(g_logits, axis=-1)
    p_grp = jnp.take_along_axis(jax.nn.softmax(g_logits, -1), grp[:, None], axis=-1)
    e_logits = ((xt @ re_w).astype(f32) + re_b).reshape(t, N_GROUPS, EXPERTS_PER_GROUP)
    e_logits = jnp.take_along_axis(e_logits, grp[:, None, None], axis=1)[:, 0]
    top_vals, top_idx = lax.top_k(e_logits, TOP_K)
    gate_w = jax.nn.softmax(top_vals, axis=-1) * p_grp
    expert_idx = grp[:, None].astype(jnp.int32) * EXPERTS_PER_GROUP + top_idx.astype(jnp.int32)
    y = grouped_expert_ffn(xt, expert_idx, w_gate, w_up, w_down)
    out = jnp.einsum('tk,tkd->td', gate_w, y.astype(f32))
    return out.astype(x.dtype).reshape(b, s, d)


def setup_inputs(seed: int = 0) -> dict:
    key = jax.random.key(seed)
    ks = jax.random.split(key, 21)
    L, D = DEPTH, D_MODEL

    def nrm(k, shape, scale):
        return jax.random.normal(k, shape, jnp.float32) * scale

    return {
        'x': nrm(ks[0], (BATCH, SEQ, D), 1.0),
        'w_in': nrm(ks[1], (L, D, IN_COLS), D ** -0.5),
        'b_fox_f': jax.random.uniform(ks[2], (L, FOX_HEADS), jnp.float32, 1.0, 5.0),
        'fox_q_gain': 1.0 + nrm(ks[3], (L, FOX_DH), 0.02),
        'fox_k_gain': 1.0 + nrm(ks[4], (L, FOX_DH), 0.02),
        'hg_lb_logits': nrm(ks[5], (L, HG_KEY_W), 0.1),
        'hg_norm_gain': 1.0 + nrm(ks[6], (L, HG_DV), 0.02),
        'w_br_a': nrm(ks[7], (L, HG_VAL_W, D), BETA * HG_VAL_W ** -0.5),
        'w_br_b': nrm(ks[8], (L, FOX_W, D), BETA * FOX_W ** -0.5),
        'w_out': nrm(ks[9], (L, D, D), BETA * D ** -0.5),
        'ln1_g': 1.0 + nrm(ks[10], (L, D), 0.02),
        'ln1_b': nrm(ks[11], (L, D), 0.02),
        'router_g_w': nrm(ks[12], (L, D, N_GROUPS), D ** -0.5),
        'router_g_b': nrm(ks[13], (L, N_GROUPS), 0.01),
        'router_e_w': nrm(ks[14], (L, D, N_EXPERTS), D ** -0.5),
        'router_e_b': nrm(ks[15], (L, N_EXPERTS), 0.01),
        'w_gate': nrm(ks[16], (L, N_EXPERTS, D, D_EXPERT), D ** -0.5),
        'w_up': nrm(ks[17], (L, N_EXPERTS, D, D_EXPERT), D ** -0.5),
        'w_down': nrm(ks[18], (L, N_EXPERTS, D_EXPERT, D), BETA * D_EXPERT ** -0.5),
        'ln2_g': 1.0 + nrm(ks[19], (L, D), 0.02),
        'ln2_b': nrm(ks[20], (L, D), 0.02),
    }


def reference(x, w_in, b_fox_f, fox_q_gain, fox_k_gain, hg_lb_logits, hg_norm_gain, w_br_a, w_br_b, w_out,
              ln1_g, ln1_b, router_g_w, router_g_b, router_e_w, router_e_b, w_gate, w_up, w_down, ln2_g, ln2_b):
    lb_p = jax.nn.softmax(hg_lb_logits.astype(jnp.float32), axis=0)
    lb_all = jnp.cumsum(lb_p, axis=0) - lb_p[0]
    for l in range(DEPTH):
        h = hybrid_mixer(x, w_in[l], b_fox_f[l], fox_q_gain[l], fox_k_gain[l], lb_all[l], hg_norm_gain[l],
                         w_br_a[l], w_br_b[l], w_out[l])
        x = layer_norm(ALPHA * x + h, ln1_g[l], ln1_b[l])
        f = hier_moe(x, router_g_w[l], router_g_b[l], router_e_w[l], router_e_b[l], w_gate[l], w_up[l], w_down[l])
        x = layer_norm(ALPHA * x + f, ln2_g[l], ln2_b[l])
    return x
```

```python
import functools

import jax
import jax.numpy as jnp
from jax import lax
from jax.experimental import pallas as pl
from jax.experimental.pallas import tpu as pltpu

F32 = jnp.float32
BF16 = jnp.bfloat16

HEAD_DIM = 128
FOX_F_COLS = 8
N_GROUPS = 4
EXPERTS_PER_GROUP = 8
N_EXPERTS = N_GROUPS * EXPERTS_PER_GROUP
TOP_K = 2
LN_EPS = 1e-5
NORM_EPS = 1e-6

LANES = 128
SUBLANES = 8
VMEM_LIMIT_BYTES = 48 * 1024 * 1024

PROJ_TM = 1024
HG_CHUNK = 64
HG_SUB = 8
HG_TS = 256
FOX_T = 512
MERGE_TM = 512
MOE_BLK = 256
COMB_TM = 512


def _cparams(sem):
    return pltpu.CompilerParams(dimension_semantics=sem, vmem_limit_bytes=VMEM_LIMIT_BYTES)


def _split3(a):
    hi = a.astype(BF16)
    r1 = a - hi.astype(F32)
    mid = r1.astype(BF16)
    lo = (r1 - mid.astype(F32)).astype(BF16)
    return hi, mid, lo


def _dot(a, b):
    return jnp.dot(a, b, preferred_element_type=F32)


def _dot_nt(a, b):
    return lax.dot_general(a, b, (((1,), (1,)), ((), ())), preferred_element_type=F32)


def _dot_tn(a, b):
    return lax.dot_general(a, b, (((0,), (0,)), ((), ())), preferred_element_type=F32)


def _proj_kernel(kind, x_ref, w_ref, *refs):
    acc = _dot(x_ref[...], w_ref[...])
    if kind == "plain":
        (o_ref,) = refs
        o_ref[...] = acc.astype(o_ref.dtype)
    elif kind == "silu":
        (o_ref,) = refs
        o_ref[...] = (acc * jax.nn.sigmoid(acc)).astype(o_ref.dtype)
    elif kind == "sigmoid":
        (o_ref,) = refs
        o_ref[...] = jax.nn.sigmoid(acc).astype(o_ref.dtype)
    elif kind == "forget":
        lb_ref, logf_ref, k_ref = refs
        lb = lb_ref[...]
        f = lb + (1.0 - lb) * jax.nn.sigmoid(acc)
        logf_ref[...] = jnp.log(f)
        k_ref[...] = (1.0 - f).astype(k_ref.dtype)
    elif kind == "rms":
        gain_ref, o_ref = refs
        for h in range(acc.shape[1] // HEAD_DIM):
            sl = slice(h * HEAD_DIM, (h + 1) * HEAD_DIM)
            a = acc[:, sl]
            ms = jnp.mean(a * a, axis=-1, keepdims=True)
            o_ref[:, sl] = (a * lax.rsqrt(ms + NORM_EPS) * gain_ref[:, sl]).astype(o_ref.dtype)
    else:
        raise ValueError(kind)


def _proj(x_bf, w_bf, kind, aux=(), out_dtypes=(BF16,)):
    t, d = x_bf.shape
    n = w_bf.shape[1]
    tm = min(PROJ_TM, t)
    in_specs = [pl.BlockSpec((tm, d), lambda i: (i, 0)), pl.BlockSpec((d, n), lambda i: (0, 0))]
    in_specs += [pl.BlockSpec((1, n), lambda i: (0, 0)) for _ in aux]
    out_specs = [pl.BlockSpec((tm, n), lambda i: (i, 0)) for _ in out_dtypes]
    out_shape = [jax.ShapeDtypeStruct((t, n), dt) for dt in out_dtypes]
    outs = pl.pallas_call(
        functools.partial(_proj_kernel, kind),
        grid=(t // tm,),
        in_specs=in_specs,
        out_specs=out_specs,
        out_shape=out_shape,
        compiler_params=_cparams(("parallel",)),
        name=f"proj_{kind}",
    )(x_bf, w_bf, *aux)
    return outs


def _fox_forget_kernel(tiles_per_seq, x_ref, w_ref, b_ref, c_ref, carry_ref):
    i = pl.program_id(0)

    @pl.when(i % tiles_per_seq == 0)
    def _():
        carry_ref[...] = jnp.zeros_like(carry_ref)

    z = _dot(x_ref[...], w_ref[...]) + b_ref[...]
    lf = jnp.minimum(z, 0.0) - jnp.log(1.0 + jnp.exp(-jnp.abs(z)))
    tm = lf.shape[0]
    row = lax.broadcasted_iota(jnp.int32, (tm, tm), 0)
    col = lax.broadcasted_iota(jnp.int32, (tm, tm), 1)
    tri = jnp.where(col <= row, 1.0, 0.0).astype(BF16)
    hi, mid, lo = _split3(lf)
    c = _dot(tri, hi) + _dot(tri, mid) + _dot(tri, lo) + carry_ref[...]
    c_ref[...] = c
    carry_ref[...] = c[tm - 1:tm, :]


def _fox_forget(x_bf, w_pad_bf, b_pad, seq):
    t, d = x_bf.shape
    tm = min(512, seq)
    return pl.pallas_call(
        functools.partial(_fox_forget_kernel, seq // tm),
        grid=(t // tm,),
        in_specs=[pl.BlockSpec((tm, d), lambda i: (i, 0)),
                  pl.BlockSpec((d, LANES), lambda i: (0, 0)),
                  pl.BlockSpec((1, LANES), lambda i: (0, 0))],
        out_specs=pl.BlockSpec((tm, LANES), lambda i: (i, 0)),
        out_shape=jax.ShapeDtypeStruct((t, LANES), F32),
        scratch_shapes=[pltpu.VMEM((1, LANES), F32)],
        compiler_params=_cparams(("arbitrary",)),
        name="fox_forget_cumsum",
    )(x_bf, w_pad_bf, b_pad)


def _bcast_row(a, period, row):
    n, w = a.shape
    a3 = a.reshape(n // period, period, w)
    return jnp.broadcast_to(a3[:, row:row + 1, :], a3.shape).reshape(n, w)


def _hgrn_kernel(q_ref, k_ref, v_ref, g_ref, hg_ref, gain_ref, tri_ref, u_ref, o_ref, st_ref):
    ts = q_ref.shape[0]
    n_chunks = ts // HG_CHUNK

    @pl.when(pl.program_id(2) == 0)
    def _():
        st_ref[...] = jnp.zeros_like(st_ref)

    q = q_ref[...].astype(F32)
    k = k_ref[...].astype(F32)
    v = v_ref[...]
    g_hi, g_mid, g_lo = _split3(g_ref[...])
    tri = tri_ref[...]
    b = _dot(tri, g_hi) + _dot(tri, g_mid) + _dot(tri, g_lo)

    row128 = lax.broadcasted_iota(jnp.int32, (ts, HEAD_DIM), 0)
    row64 = lax.broadcasted_iota(jnp.int32, (ts, HG_CHUNK), 0) % HG_CHUNK
    col64 = lax.broadcasted_iota(jnp.int32, (ts, HG_CHUNK), 1)

    parts = []
    for s in range(HG_SUB):
        b_s = _bcast_row(b, HG_SUB, s)
        k_s = _bcast_row(k, HG_SUB, s)
        decay = jnp.exp(jnp.minimum(b - b_s, 0.0))
        parts.append((q * k_s * decay).astype(BF16))
    scores = _dot(jnp.concatenate(parts, axis=1), u_ref[...])
    same_sub = (row64 // HG_SUB) == (col64 // HG_SUB)
    scores = jnp.where(same_sub & ((col64 % HG_SUB) <= (row64 % HG_SUB)), scores, 0.0)

    level_q, level_k, level_mask = [], [], []
    m = HG_SUB
    while m < HG_CHUNK:
        ref_b = _bcast_row(b, 2 * m, m - 1)
        second = (row128 % (2 * m)) >= m
        e = jnp.exp(jnp.where(second, b - ref_b, ref_b - b))
        level_q.append((q * e).astype(BF16))
        level_k.append((k * e).astype(BF16))
        level_mask.append(((row64 // (2 * m)) == (col64 // (2 * m)))
                          & ((row64 % (2 * m)) >= m) & ((col64 % (2 * m)) < m))
        m *= 2

    b_last = _bcast_row(b, HG_CHUNK, HG_CHUNK - 1)
    q_in = (q * jnp.exp(b)).astype(BF16)
    k_out = (k * jnp.exp(b_last - b)).astype(BF16)
    chunk_decay = jnp.exp(b_last)

    gain = gain_ref[...]
    for c in range(n_chunks):
        sl = slice(c * HG_CHUNK, (c + 1) * HG_CHUNK)
        a = scores[sl]
        for lq, lk, lm in zip(level_q, level_k, level_mask):
            a = a + jnp.where(lm[sl], _dot_nt(lq[sl], lk[sl]), 0.0)
        st = st_ref[...]
        o = _dot(a.astype(BF16), v[sl]) + _dot_nt(q_in[sl], st.astype(BF16))
        st_ref[...] = st * chunk_decay[c * HG_CHUNK:c * HG_CHUNK + 1, :] + _dot_tn(v[sl], k_out[sl])
        ms = jnp.mean(o * o, axis=-1, keepdims=True)
        o = o * lax.rsqrt(ms + NORM_EPS) * gain * hg_ref[sl, :].astype(F32)
        o_ref[sl, :] = o.astype(o_ref.dtype)


def _hgrn(q, k, v, logf, hg_act, gain, batch, seq):
    d = q.shape[-1]
    heads = d // HEAD_DIM
    ts = min(HG_TS, seq)
    shp = (batch, seq, d)
    q, k, v, logf, hg_act = (a.reshape(shp) for a in (q, k, v, logf, hg_act))
    r = jnp.arange(ts)
    tri = ((r[:, None] // HG_CHUNK == r[None, :] // HG_CHUNK) & (r[None, :] <= r[:, None])).astype(BF16)
    u_rows = jnp.arange(HG_SUB * HEAD_DIM) // HEAD_DIM
    u = (u_rows[:, None] == (jnp.arange(HG_CHUNK)[None, :] % HG_SUB)).astype(BF16)
    tile = pl.BlockSpec((None, ts, HEAD_DIM), lambda b, h, s: (b, s, h))
    out = pl.pallas_call(
        _hgrn_kernel,
        grid=(batch, heads, seq // ts),
        in_specs=[tile, tile, tile, tile, tile,
                  pl.BlockSpec((1, HEAD_DIM), lambda b, h, s: (0, 0)),
                  pl.BlockSpec((ts, ts), lambda b, h, s: (0, 0)),
                  pl.BlockSpec((HG_SUB * HEAD_DIM, HG_CHUNK), lambda b, h, s: (0, 0))],
        out_specs=tile,
        out_shape=jax.ShapeDtypeStruct(shp, BF16),
        scratch_shapes=[pltpu.VMEM((HEAD_DIM, HEAD_DIM), F32)],
        compiler_params=_cparams(("parallel", "parallel", "arbitrary")),
        name="hgrn2",
    )(q, k, v, logf, hg_act, gain, tri, u)
    return out.reshape(batch * seq, d)


def _fox_kernel(q_ref, k_ref, v_ref, nc_ref, o_ref, m_ref, l_ref, acc_ref):
    tq = q_ref.shape[0]
    qi = pl.program_id(2)
    q = q_ref[...]
    m_ref[...] = jnp.full_like(m_ref, -jnp.inf)
    l_ref[...] = jnp.zeros_like(l_ref)
    acc_ref[...] = jnp.zeros_like(acc_ref)

    def step(j, masked):
        start = pl.multiple_of(j * tq, tq)
        ks = k_ref[pl.ds(start, tq), :]
        vs = v_ref[pl.ds(start, tq), :]
        s = _dot_nt(q, ks) + nc_ref[j]
        if masked:
            row = lax.broadcasted_iota(jnp.int32, s.shape, 0)
            col = lax.broadcasted_iota(jnp.int32, s.shape, 1)
            s = jnp.where(col <= row, s, -jnp.inf)
        m_old = m_ref[...]
        m_new = jnp.maximum(m_old, jnp.max(s, axis=-1, keepdims=True))
        alpha = jnp.exp(m_old - m_new)
        p = jnp.exp(s - m_new)
        l_ref[...] = alpha * l_ref[...] + jnp.sum(p, axis=-1, keepdims=True)
        acc_ref[...] = alpha * acc_ref[...] + _dot(p.astype(BF16), vs)
        m_ref[...] = m_new

    def body(j, carry):
        step(j, False)
        return carry

    lax.fori_loop(0, qi, body, 0)
    step(qi, True)
    o_ref[...] = (acc_ref[...] / l_ref[...]).astype(o_ref.dtype)


def _fox(q, k, v, neg_c, batch, seq):
    d = q.shape[-1]
    heads = d // HEAD_DIM
    t = min(FOX_T, seq)
    shp = (batch, seq, d)
    q, k, v = (a.reshape(shp) for a in (q, k, v))
    qtile = pl.BlockSpec((None, t, HEAD_DIM), lambda b, h, i: (b, i, h))
    kvfull = pl.BlockSpec((None, seq, HEAD_DIM), lambda b, h, i: (b, 0, h))
    out = pl.pallas_call(
        _fox_kernel,
        grid=(batch, heads, seq // t),
        in_specs=[qtile, kvfull, kvfull,
                  pl.BlockSpec((None, None, seq // t, 1, t), lambda b, h, i: (b, h, 0, 0, 0))],
        out_specs=qtile,
        out_shape=jax.ShapeDtypeStruct(shp, BF16),
        scratch_shapes=[pltpu.VMEM((t, 1), F32), pltpu.VMEM((t, 1), F32), pltpu.VMEM((t, HEAD_DIM), F32)],
        compiler_params=_cparams(("parallel", "parallel", "arbitrary")),
        name="fox_attention",
    )(q, k, v, neg_c)
    return out.reshape(batch * seq, d)


def _layer_norm(z, g, b):
    mu = jnp.mean(z, axis=-1, keepdims=True)
    zc = z - mu
    var = jnp.mean(zc * zc, axis=-1, keepdims=True)
    return zc * lax.rsqrt(var + LN_EPS) * g + b


def _first_index_of_max(vals, lane):
    vmax = jnp.max(vals, axis=-1, keepdims=True)
    idx = jnp.min(jnp.where(vals == vmax, lane, LANES), axis=-1, keepdims=True)
    return vmax, idx


def _merge_kernel(alpha, oa_ref, ob_ref, ga_ref, gb_ref, x_ref, wa_ref, wb_ref, wo_ref, g_ref, b_ref,
                  wr_hi_ref, wr_lo_ref, br_ref, x1_ref, gate_ref, idx_ref):
    y_a = _dot(oa_ref[...], wa_ref[...])
    y_b = _dot(ob_ref[...], wb_ref[...])
    merged = ga_ref[...].astype(F32) * y_a + gb_ref[...].astype(F32) * y_b
    h = _dot(merged.astype(BF16), wo_ref[...])
    x1 = _layer_norm(alpha * x_ref[...] + h, g_ref[...], b_ref[...])
    x1_ref[...] = x1

    x_hi = x1.astype(BF16)
    x_lo = (x1 - x_hi.astype(F32)).astype(BF16)
    wr_hi = wr_hi_ref[...]
    logits = _dot(x_hi, wr_hi) + _dot(x_lo, wr_hi) + _dot(x_hi, wr_lo_ref[...]) + br_ref[...]
    lane = lax.broadcasted_iota(jnp.int32, logits.shape, 1)
    neg = -jnp.inf
    g_logits = jnp.where(lane < N_GROUPS, logits, neg)
    g_max, grp = _first_index_of_max(g_logits, lane)
    p_grp = 1.0 / jnp.sum(jnp.exp(g_logits - g_max), axis=-1, keepdims=True)
    lo = N_GROUPS + EXPERTS_PER_GROUP * grp
    e_logits = jnp.where((lane >= lo) & (lane < lo + EXPERTS_PER_GROUP), logits, neg)
    v1, i1 = _first_index_of_max(e_logits, lane)
    v2, i2 = _first_index_of_max(jnp.where(lane == i1, neg, e_logits), lane)
    e2 = jnp.exp(v2 - v1)
    w1 = 1.0 / (1.0 + e2)
    gate_ref[...] = jnp.where(lane == 0, w1 * p_grp, jnp.where(lane == 1, e2 * w1 * p_grp, 0.0))
    idx_ref[...] = jnp.where(lane == 0, i1 - N_GROUPS, jnp.where(lane == 1, i2 - N_GROUPS, 0))


def _merge(alpha, o_a, o_b, ga, gb, x, wa, wb, wo, ln_g, ln_b, wr_hi, wr_lo, br):
    t, d = x.shape
    tm = min(MERGE_TM, t)
    row = lambda n: pl.BlockSpec((tm, n), lambda i: (i, 0))
    full = lambda a: pl.BlockSpec(a.shape, lambda i: (0, 0))
    return pl.pallas_call(
        functools.partial(_merge_kernel, alpha),
        grid=(t // tm,),
        in_specs=[row(d), row(d), row(d), row(d), row(d), full(wa), full(wb), full(wo), full(ln_g), full(ln_b),
                  full(wr_hi), full(wr_lo), full(br)],
        out_specs=[row(d), row(LANES), row(LANES)],
        out_shape=[jax.ShapeDtypeStruct((t, d), F32), jax.ShapeDtypeStruct((t, LANES), F32),
                   jax.ShapeDtypeStruct((t, LANES), jnp.int32)],
        compiler_params=_cparams(("parallel",)),
        name="merge_ln_router",
    )(o_a, o_b, ga, gb, x, wa, wb, wo, ln_g, ln_b, wr_hi, wr_lo, br)


def _moe_kernel(blk_e_ref, n_used_ref, src_ref, dst_ref, x_hbm, wg_ref, wu_ref, wd_ref, y_hbm,
                xbuf, ybuf, sems):
    i = pl.program_id(0)
    blk = xbuf.shape[0]

    def gather_copy(r):
        return pltpu.make_async_copy(x_hbm.at[pl.ds(src_ref[0, r], 1), :], xbuf.at[pl.ds(r, 1), :], sems.at[0])

    def scatter_copy(r):
        return pltpu.make_async_copy(ybuf.at[pl.ds(r, 1), :], y_hbm.at[pl.ds(dst_ref[0, r], 1), :], sems.at[1])

    def for_rows(fn):
        def body(r, carry):
            fn(r)
            return carry
        lax.fori_loop(0, blk, body, 0)

    @pl.when(i < n_used_ref[0])
    def _():
        for_rows(lambda r: gather_copy(r).start())
        for_rows(lambda r: gather_copy(r).wait())
        xb = xbuf[...].astype(BF16)
        gate = _dot(xb, wg_ref[...])
        up = _dot(xb, wu_ref[...])
        hidden = (gate * jax.nn.sigmoid(gate) * up).astype(BF16)
        ybuf[...] = _dot(hidden, wd_ref[...])
        for_rows(lambda r: scatter_copy(r).start())
        for_rows(lambda r: scatter_copy(r).wait())

    @pl.when(i >= n_used_ref[0])
    def _():
        ybuf[...] = jnp.zeros_like(ybuf)
        fill = pltpu.make_async_copy(ybuf, y_hbm.at[pl.ds(pl.multiple_of(i * blk, blk), blk), :], sems.at[1])
        fill.start()
        fill.wait()


def _moe(x1, blk_e, n_used, src, dst, wg, wu, wd):
    t, d = x1.shape
    n_blocks = blk_e.shape[0]
    blk = src.shape[-1]
    de = wg.shape[-1]
    idx_spec = pl.BlockSpec((None, 1, blk), lambda i, be, nu: (i, 0, 0), memory_space=pltpu.SMEM)
    grid_spec = pltpu.PrefetchScalarGridSpec(
        num_scalar_prefetch=2,
        grid=(n_blocks,),
        in_specs=[idx_spec, idx_spec,
                  pl.BlockSpec(memory_space=pl.ANY),
                  pl.BlockSpec((None, d, de), lambda i, be, nu: (be[i], 0, 0)),
                  pl.BlockSpec((None, d, de), lambda i, be, nu: (be[i], 0, 0)),
                  pl.BlockSpec((None, de, d), lambda i, be, nu: (be[i], 0, 0))],
        out_specs=pl.BlockSpec(memory_space=pl.ANY),
        scratch_shapes=[pltpu.VMEM((blk, d), F32), pltpu.VMEM((blk, d), F32), pltpu.SemaphoreType.DMA((2,))],
    )
    return pl.pallas_call(
        _moe_kernel,
        grid_spec=grid_spec,
        out_shape=jax.ShapeDtypeStruct((n_blocks * blk, d), F32),
        compiler_params=_cparams(("arbitrary",)),
        name="moe_experts",
    )(blk_e, n_used, src, dst, x1, wg, wu, wd)


def _moe_plan(expert_idx, t):
    a = t * TOP_K
    blk = min(MOE_BLK, a)
    n_blocks = a // blk + N_EXPERTS
    n_slots = n_blocks * blk
    flat_e = expert_idx.reshape(a)
    order = jnp.argsort(flat_e).astype(jnp.int32)
    counts = jnp.bincount(flat_e, length=N_EXPERTS).astype(jnp.int32)
    padded = (counts + blk - 1) // blk * blk
    pad_end = jnp.cumsum(padded)
    pad_start = pad_end - padded
    start = jnp.cumsum(counts) - counts
    n_used = (pad_end[-1] // blk).astype(jnp.int32)
    block_start = jnp.arange(n_blocks, dtype=jnp.int32) * blk
    blk_e = jnp.minimum(jnp.searchsorted(pad_end, block_start, side="right"), N_EXPERTS - 1).astype(jnp.int32)
    blk_e = jnp.where(jnp.arange(n_blocks) < n_used, blk_e, blk_e[jnp.maximum(n_used - 1, 0)])
    slot = jnp.arange(n_slots, dtype=jnp.int32)
    slot_e = jnp.repeat(blk_e, blk)
    r = slot - pad_start[slot_e]
    valid = (r < counts[slot_e]) & (slot < pad_end[-1])
    assign = order[jnp.clip(start[slot_e] + r, 0, a - 1)]
    src = jnp.where(valid, assign // TOP_K, 0)
    pad_rank = jnp.cumsum(jnp.where(valid, 0, 1)) - 1
    dst = jnp.where(valid, (assign % TOP_K) * t + assign // TOP_K,
                    jnp.where(slot >= pad_end[-1], slot, a + pad_rank))
    return (blk_e, n_used.reshape(1), src.reshape(n_blocks, 1, blk).astype(jnp.int32),
            dst.reshape(n_blocks, 1, blk).astype(jnp.int32))


def _combine_kernel(alpha, x_ref, y0_ref, y1_ref, gate_ref, g_ref, b_ref, o_ref, obf_ref):
    gate = gate_ref[...]
    f = gate[:, 0:1] * y0_ref[...] + gate[:, 1:2] * y1_ref[...]
    out = _layer_norm(alpha * x_ref[...] + f, g_ref[...], b_ref[...])
    o_ref[...] = out
    obf_ref[...] = out.astype(BF16)


def _combine(alpha, x1, y, gate, ln_g, ln_b):
    t, d = x1.shape
    tm = min(COMB_TM, t)
    nt = t // tm
    row = lambda n: pl.BlockSpec((tm, n), lambda i: (i, 0))
    full = lambda a: pl.BlockSpec(a.shape, lambda i: (0, 0))
    return pl.pallas_call(
        functools.partial(_combine_kernel, alpha),
        grid=(nt,),
        in_specs=[row(d), row(d), pl.BlockSpec((tm, d), lambda i: (nt + i, 0)), row(LANES), full(ln_g), full(ln_b)],
        out_specs=[row(d), row(d)],
        out_shape=[jax.ShapeDtypeStruct((t, d), F32), jax.ShapeDtypeStruct((t, d), BF16)],
        compiler_params=_cparams(("parallel",)),
        name="combine_ln",
    )(x1, y, y, gate, ln_g, ln_b)


def kernel(x, w_in, b_fox_f, fox_q_gain, fox_k_gain, hg_lb_logits, hg_norm_gain, w_br_a, w_br_b, w_out, ln1_g, ln1_b, router_g_w, router_g_b, router_e_w, router_e_b, w_gate, w_up, w_down, ln2_g, ln2_b):
    batch, seq, d = x.shape
    depth = w_in.shape[0]
    t = batch * seq
    heads = d // HEAD_DIM
    alpha = (2 * depth) ** 0.25
    assert w_in.shape[2] == 9 * d + FOX_F_COLS

    lb_p = jax.nn.softmax(hg_lb_logits.astype(F32), axis=0)
    lb_all = jnp.cumsum(lb_p, axis=0) - lb_p[0]
    fox_scale = HEAD_DIM ** -0.5

    xf = x.reshape(t, d)
    xb = xf.astype(BF16)
    for l in range(depth):
        w = w_in[l].astype(BF16)
        col = lambda j: w[:, j * d:(j + 1) * d]
        w_gates = w[:, 7 * d + FOX_F_COLS:]
        w_ff = jnp.pad(w[:, 7 * d:7 * d + FOX_F_COLS], ((0, 0), (0, LANES - FOX_F_COLS)))
        b_ff = jnp.pad(b_fox_f[l], (0, LANES - FOX_F_COLS)).reshape(1, LANES)

        (q_a,) = _proj(xb, col(0), "silu")
        logf, k_a = _proj(xb, col(1), "forget", aux=(lb_all[l].reshape(1, d),), out_dtypes=(F32, BF16))
        (v_a,) = _proj(xb, col(2), "plain")
        (hg_act,) = _proj(xb, col(3), "silu")
        (q_b,) = _proj(xb, col(4), "rms", aux=(jnp.tile(fox_q_gain[l] * fox_scale, heads).reshape(1, d),))
        (k_b,) = _proj(xb, col(5), "rms", aux=(jnp.tile(fox_k_gain[l], heads).reshape(1, d),))
        (v_b,) = _proj(xb, col(6), "plain")
        (g_a,) = _proj(xb, w_gates[:, :d], "sigmoid")
        (g_b,) = _proj(xb, w_gates[:, d:], "sigmoid")
        c = _fox_forget(xb, w_ff, b_ff, seq)

        o_a = _hgrn(q_a, k_a, v_a, logf, hg_act, hg_norm_gain[l].reshape(1, HEAD_DIM), batch, seq)

        ft = min(FOX_T, seq)
        neg_c = -c[:, :heads].reshape(batch, seq, heads).transpose(0, 2, 1).reshape(batch, heads, seq // ft, 1, ft)
        o_b = _fox(q_b, k_b, v_b, neg_c, batch, seq)

        w_r = jnp.pad(jnp.concatenate([router_g_w[l], router_e_w[l]], axis=1),
                      ((0, 0), (0, LANES - N_GROUPS - N_EXPERTS)))
        b_r = jnp.pad(jnp.concatenate([router_g_b[l], router_e_b[l]]), (0, LANES - N_GROUPS - N_EXPERTS)).reshape(1, LANES)
        w_r_hi = w_r.astype(BF16)
        w_r_lo = (w_r - w_r_hi.astype(F32)).astype(BF16)
        x1, gate, idx = _merge(alpha, o_a, o_b, g_a, g_b, xf, w_br_a[l].astype(BF16), w_br_b[l].astype(BF16),
                               w_out[l].astype(BF16), ln1_g[l].reshape(1, d), ln1_b[l].reshape(1, d),
                               w_r_hi, w_r_lo, b_r)

        blk_e, n_used, src, dst = _moe_plan(idx[:, :TOP_K], t)
        y = _moe(x1, blk_e, n_used, src, dst, w_gate[l].astype(BF16), w_up[l].astype(BF16), w_down[l].astype(BF16))
        xf, xb = _combine(alpha, x1, y, gate, ln2_g[l].reshape(1, d), ln2_b[l].reshape(1, d))
    return xf.reshape(batch, seq, d)
```

```python
import functools

import jax
import jax.numpy as jnp
from jax import lax
from jax.experimental import pallas as pl
from jax.experimental.pallas import tpu as pltpu

F32 = jnp.float32
BF16 = jnp.bfloat16

HEAD_DIM = 128
FOX_F_COLS = 8
N_GROUPS = 4
EXPERTS_PER_GROUP = 8
N_EXPERTS = N_GROUPS * EXPERTS_PER_GROUP
TOP_K = 2
LN_EPS = 1e-5
NORM_EPS = 1e-6
LOG2E = 1.4426950408889634
FOX_MASKED = -1e30

LANES = 128
SUBLANES = 8
VMEM_LIMIT_BYTES = 48 * 1024 * 1024

PROJ_TM = 1024
HG_CHUNK = 64
HG_SUB = 8
HG_TS = 256
FOX_TQ = 512
FOX_TK = 256
FOX_CW = 256
MERGE_TM = 512
MOE_BLK = 256
COMB_TM = 512


def _cparams(sem):
    return pltpu.CompilerParams(dimension_semantics=sem, vmem_limit_bytes=VMEM_LIMIT_BYTES)


def _split3(a):
    hi = a.astype(BF16)
    r1 = a - hi.astype(F32)
    mid = r1.astype(BF16)
    lo = (r1 - mid.astype(F32)).astype(BF16)
    return hi, mid, lo


def _dot(a, b):
    return jnp.dot(a, b, preferred_element_type=F32)


def _dot_nt(a, b):
    return lax.dot_general(a, b, (((1,), (1,)), ((), ())), preferred_element_type=F32)


def _dot_tn(a, b):
    return lax.dot_general(a, b, (((0,), (0,)), ((), ())), preferred_element_type=F32)


def _proj_kernel(kind, x_ref, w_ref, *refs):
    acc = _dot(x_ref[...], w_ref[...])
    if kind == "plain":
        (o_ref,) = refs
        o_ref[...] = acc.astype(o_ref.dtype)
    elif kind == "silu":
        (o_ref,) = refs
        o_ref[...] = (acc * jax.nn.sigmoid(acc)).astype(o_ref.dtype)
    elif kind == "sigmoid":
        (o_ref,) = refs
        o_ref[...] = jax.nn.sigmoid(acc).astype(o_ref.dtype)
    elif kind == "forget":
        lb_ref, logf_ref, k_ref = refs
        lb = lb_ref[...]
        f = lb + (1.0 - lb) * jax.nn.sigmoid(acc)
        logf_ref[...] = jnp.log(f)
        k_ref[...] = (1.0 - f).astype(k_ref.dtype)
    elif kind == "rms":
        gain_ref, o_ref = refs
        for h in range(acc.shape[1] // HEAD_DIM):
            sl = slice(h * HEAD_DIM, (h + 1) * HEAD_DIM)
            a = acc[:, sl]
            ms = jnp.mean(a * a, axis=-1, keepdims=True)
            o_ref[:, sl] = (a * lax.rsqrt(ms + NORM_EPS) * gain_ref[:, sl]).astype(o_ref.dtype)
    else:
        raise ValueError(kind)


def _proj(x_bf, w_bf, kind, aux=(), out_dtypes=(BF16,)):
    t, d = x_bf.shape
    n = w_bf.shape[1]
    tm = min(PROJ_TM, t)
    in_specs = [pl.BlockSpec((tm, d), lambda i: (i, 0)), pl.BlockSpec((d, n), lambda i: (0, 0))]
    in_specs += [pl.BlockSpec((1, n), lambda i: (0, 0)) for _ in aux]
    out_specs = [pl.BlockSpec((tm, n), lambda i: (i, 0)) for _ in out_dtypes]
    out_shape = [jax.ShapeDtypeStruct((t, n), dt) for dt in out_dtypes]
    outs = pl.pallas_call(
        functools.partial(_proj_kernel, kind),
        grid=(t // tm,),
        in_specs=in_specs,
        out_specs=out_specs,
        out_shape=out_shape,
        compiler_params=_cparams(("parallel",)),
        name=f"proj_{kind}",
    )(x_bf, w_bf, *aux)
    return outs


def _fox_forget_kernel(tiles_per_seq, x_ref, w_ref, b_ref, nc_ref, carry_ref):
    i = pl.program_id(0)

    @pl.when(i % tiles_per_seq == 0)
    def _():
        carry_ref[...] = jnp.zeros_like(carry_ref)

    z = _dot(x_ref[...], w_ref[...]) + b_ref[...]
    lf = jnp.minimum(z, 0.0) - jnp.log(1.0 + jnp.exp(-jnp.abs(z)))
    tm = lf.shape[0]
    row = lax.broadcasted_iota(jnp.int32, (tm, tm), 0)
    col = lax.broadcasted_iota(jnp.int32, (tm, tm), 1)
    tri = jnp.where(col <= row, 1.0, 0.0).astype(BF16)
    hi, mid, lo = _split3(lf)
    c = _dot(tri, hi) + _dot(tri, mid) + _dot(tri, lo) + carry_ref[...]
    carry_ref[...] = c[tm - 1:tm, :]
    for h in range(nc_ref.shape[1] // HEAD_DIM):
        nc_ref[:, h * HEAD_DIM:(h + 1) * HEAD_DIM] = jnp.broadcast_to(c[:, h:h + 1] * (-LOG2E), (tm, HEAD_DIM))


def _fox_forget(x_bf, w_pad_bf, b_pad, seq):
    t, d = x_bf.shape
    tm = min(512, seq)
    return pl.pallas_call(
        functools.partial(_fox_forget_kernel, seq // tm),
        grid=(t // tm,),
        in_specs=[pl.BlockSpec((tm, d), lambda i: (i, 0)),
                  pl.BlockSpec((d, LANES), lambda i: (0, 0)),
                  pl.BlockSpec((1, LANES), lambda i: (0, 0))],
        out_specs=pl.BlockSpec((tm, FOX_F_COLS * HEAD_DIM), lambda i: (i, 0)),
        out_shape=jax.ShapeDtypeStruct((t, FOX_F_COLS * HEAD_DIM), F32),
        scratch_shapes=[pltpu.VMEM((1, LANES), F32)],
        compiler_params=_cparams(("arbitrary",)),
        name="fox_forget_cumsum",
    )(x_bf, w_pad_bf, b_pad)


def _bcast_row(a, period, row):
    n, w = a.shape
    a3 = a.reshape(n // period, period, w)
    return jnp.broadcast_to(a3[:, row:row + 1, :], a3.shape).reshape(n, w)


def _hgrn_kernel(q_ref, k_ref, v_ref, g_ref, hg_ref, gain_ref, tri_ref, u_ref, o_ref, st_ref):
    ts = q_ref.shape[0]
    n_chunks = ts // HG_CHUNK

    @pl.when(pl.program_id(2) == 0)
    def _():
        st_ref[...] = jnp.zeros_like(st_ref)

    q = q_ref[...].astype(F32)
    k = k_ref[...].astype(F32)
    v = v_ref[...]
    g_hi, g_mid, g_lo = _split3(g_ref[...])
    tri = tri_ref[...]
    b = _dot(tri, g_hi) + _dot(tri, g_mid) + _dot(tri, g_lo)

    row128 = lax.broadcasted_iota(jnp.int32, (ts, HEAD_DIM), 0)
    row64 = lax.broadcasted_iota(jnp.int32, (ts, HG_CHUNK), 0) % HG_CHUNK
    col64 = lax.broadcasted_iota(jnp.int32, (ts, HG_CHUNK), 1)

    parts = []
    for s in range(HG_SUB):
        b_s = _bcast_row(b, HG_SUB, s)
        k_s = _bcast_row(k, HG_SUB, s)
        decay = jnp.exp(jnp.minimum(b - b_s, 0.0))
        parts.append((q * k_s * decay).astype(BF16))
    scores = _dot(jnp.concatenate(parts, axis=1), u_ref[...])
    same_sub = (row64 // HG_SUB) == (col64 // HG_SUB)
    scores = jnp.where(same_sub & ((col64 % HG_SUB) <= (row64 % HG_SUB)), scores, 0.0)

    level_q, level_k, level_mask = [], [], []
    m = HG_SUB
    while m < HG_CHUNK:
        ref_b = _bcast_row(b, 2 * m, m - 1)
        second = (row128 % (2 * m)) >= m
        e = jnp.exp(jnp.where(second, b - ref_b, ref_b - b))
        level_q.append((q * e).astype(BF16))
        level_k.append((k * e).astype(BF16))
        level_mask.append(((row64 // (2 * m)) == (col64 // (2 * m)))
                          & ((row64 % (2 * m)) >= m) & ((col64 % (2 * m)) < m))
        m *= 2

    b_last = _bcast_row(b, HG_CHUNK, HG_CHUNK - 1)
    q_in = (q * jnp.exp(b)).astype(BF16)
    k_out = (k * jnp.exp(b_last - b)).astype(BF16)
    chunk_decay = jnp.exp(b_last)

    gain = gain_ref[...]
    for c in range(n_chunks):
        sl = slice(c * HG_CHUNK, (c + 1) * HG_CHUNK)
        a = scores[sl]
        for lq, lk, lm in zip(level_q, level_k, level_mask):
            a = a + jnp.where(lm[sl], _dot_nt(lq[sl], lk[sl]), 0.0)
        st = st_ref[...]
        o = _dot(a.astype(BF16), v[sl]) + _dot_nt(q_in[sl], st.astype(BF16))
        st_ref[...] = st * chunk_decay[c * HG_CHUNK:c * HG_CHUNK + 1, :] + _dot_tn(v[sl], k_out[sl])
        ms = jnp.mean(o * o, axis=-1, keepdims=True)
        o = o * lax.rsqrt(ms + NORM_EPS) * gain * hg_ref[sl, :].astype(F32)
        o_ref[sl, :] = o.astype(o_ref.dtype)


def _hgrn(q, k, v, logf, hg_act, gain, batch, seq):
    d = q.shape[-1]
    heads = d // HEAD_DIM
    ts = min(HG_TS, seq)
    shp = (batch, seq, d)
    q, k, v, logf, hg_act = (a.reshape(shp) for a in (q, k, v, logf, hg_act))
    r = jnp.arange(ts)
    tri = ((r[:, None] // HG_CHUNK == r[None, :] // HG_CHUNK) & (r[None, :] <= r[:, None])).astype(BF16)
    u_rows = jnp.arange(HG_SUB * HEAD_DIM) // HEAD_DIM
    u = (u_rows[:, None] == (jnp.arange(HG_CHUNK)[None, :] % HG_SUB)).astype(BF16)
    tile = pl.BlockSpec((None, ts, HEAD_DIM), lambda b, h, s: (b, s, h))
    out = pl.pallas_call(
        _hgrn_kernel,
        grid=(batch, heads, seq // ts),
        in_specs=[tile, tile, tile, tile, tile,
                  pl.BlockSpec((1, HEAD_DIM), lambda b, h, s: (0, 0)),
                  pl.BlockSpec((ts, ts), lambda b, h, s: (0, 0)),
                  pl.BlockSpec((HG_SUB * HEAD_DIM, HG_CHUNK), lambda b, h, s: (0, 0))],
        out_specs=tile,
        out_shape=jax.ShapeDtypeStruct(shp, BF16),
        scratch_shapes=[pltpu.VMEM((HEAD_DIM, HEAD_DIM), F32)],
        compiler_params=_cparams(("parallel", "parallel", "arbitrary")),
        name="hgrn2",
    )(q, k, v, logf, hg_act, gain, tri, u)
    return out.reshape(batch * seq, d)


def _fox_kernel(qt_ref, k_ref, vt_ref, nc_ref, o_ref, acc_ref, sc_ref):
    tq = qt_ref.shape[1]
    tk = vt_ref.shape[2]
    n_streams = acc_ref.shape[0]
    qi = pl.program_id(2)
    cw = min(FOX_CW, tq)
    n_cg = tq // cw
    acc_ref[...] = jnp.zeros_like(acc_ref)

    chains = [(s, cg) for s in range(n_streams) for cg in range(n_cg)]

    def block_starts(g):
        return [pl.multiple_of((g * n_streams + s) * tk, tk) for s in range(n_streams)]

    def score_matmuls(g):
        starts = block_starts(g)
        for ci, (s, cg) in enumerate(chains):
            sc_ref[g % 2, ci] = _dot(k_ref[pl.ds(starts[s], tk), :], qt_ref[:, cg * cw:(cg + 1) * cw])

    def group(g, carry, masked, prefetch_scores):
        starts = block_starts(g)
        scores = [sc_ref[g % 2, ci] for ci in range(len(chains))]
        probs, ml_new = [], []
        for (s, cg), st, (m_old, l_old) in zip(chains, scores, carry):
            st = st + jnp.concatenate([nc_ref[pl.ds(starts[s], tk), :]] * (cw // HEAD_DIM), axis=1)
            if masked:
                key_pos = starts[s] + lax.broadcasted_iota(jnp.int32, st.shape, 0)
                query_pos = qi * tq + cg * cw + lax.broadcasted_iota(jnp.int32, st.shape, 1)
                st = jnp.where(key_pos <= query_pos, st, FOX_MASKED)
            m_new = jnp.maximum(m_old, jnp.max(st, axis=0, keepdims=True))
            alpha = jnp.exp2(m_old - m_new)
            p = jnp.exp2(st - m_new)
            ml_new.append((m_new, alpha * l_old + jnp.sum(p, axis=0, keepdims=True)))
            probs.append((alpha, p.astype(BF16)))
        if prefetch_scores:
            score_matmuls(g + 1)
        for (s, cg), (alpha, p) in zip(chains, probs):
            cols = slice(cg * cw, (cg + 1) * cw)
            acc_ref[s, :, cols] = acc_ref[s, :, cols] * alpha + _dot(vt_ref[g * n_streams + s], p)
        return tuple(ml_new)

    init = ((jnp.full((1, cw), FOX_MASKED, F32), jnp.zeros((1, cw), F32)),) * len(chains)

    score_matmuls(0)
    carry = lax.fori_loop(0, qi, lambda g, c: group(g, c, False, True), init)
    carry = group(qi, carry, True, False)
    carry = tuple(tuple(carry[s * n_cg + cg] for cg in range(n_cg)) for s in range(n_streams))

    for cg in range(n_cg):
        cols = slice(cg * cw, (cg + 1) * cw)
        m = carry[0][cg][0]
        for s in range(1, n_streams):
            m = jnp.maximum(m, carry[s][cg][0])
        l = jnp.zeros((1, cw), F32)
        acc = jnp.zeros((HEAD_DIM, cw), F32)
        for s in range(n_streams):
            w = jnp.exp2(carry[s][cg][0] - m)
            l = l + w * carry[s][cg][1]
            acc = acc + w * acc_ref[s, :, cols]
        o_ref[cols, :] = (acc * (1.0 / l)).T.astype(o_ref.dtype)


def _fox(q, k, v, nc_rep, batch, seq):
    d = q.shape[-1]
    heads = d // HEAD_DIM
    tq = min(FOX_TQ, seq)
    tk = min(FOX_TK, seq)
    shp = (batch, seq, d)
    qt = q.reshape(batch, seq, heads, HEAD_DIM).transpose(0, 2, 3, 1)
    vt = v.reshape(batch, seq // tk, tk, heads, HEAD_DIM).transpose(0, 3, 1, 4, 2)
    kvfull = pl.BlockSpec((None, seq, HEAD_DIM), lambda b, h, i: (b, 0, h))
    out = pl.pallas_call(
        _fox_kernel,
        grid=(batch, heads, seq // tq),
        in_specs=[pl.BlockSpec((None, None, HEAD_DIM, tq), lambda b, h, i: (b, h, 0, i)),
                  kvfull,
                  pl.BlockSpec((None, None, seq // tk, HEAD_DIM, tk), lambda b, h, i: (b, h, 0, 0, 0)),
                  kvfull],
        out_specs=pl.BlockSpec((None, tq, HEAD_DIM), lambda b, h, i: (b, i, h)),
        out_shape=jax.ShapeDtypeStruct(shp, BF16),
        scratch_shapes=[pltpu.VMEM((tq // tk, HEAD_DIM, tq), F32),
                        pltpu.VMEM((2, (tq // tk) * (tq // min(FOX_CW, tq)), tk, min(FOX_CW, tq)), F32)],
        compiler_params=_cparams(("parallel", "parallel", "arbitrary")),
        name="fox_attention",
    )(qt, k.reshape(shp), vt, nc_rep.reshape(shp))
    return out.reshape(batch * seq, d)


def _layer_norm(z, g, b):
    mu = jnp.mean(z, axis=-1, keepdims=True)
    zc = z - mu
    var = jnp.mean(zc * zc, axis=-1, keepdims=True)
    return zc * lax.rsqrt(var + LN_EPS) * g + b


def _first_index_of_max(vals, lane):
    vmax = jnp.max(vals, axis=-1, keepdims=True)
    idx = jnp.min(jnp.where(vals == vmax, lane, LANES), axis=-1, keepdims=True)
    return vmax, idx


def _merge_kernel(alpha, oa_ref, ob_ref, ga_ref, gb_ref, x_ref, wa_ref, wb_ref, wo_ref, g_ref, b_ref,
                  wr_hi_ref, wr_lo_ref, br_ref, x1_ref, gate_ref, idx_ref):
    y_a = _dot(oa_ref[...], wa_ref[...])
    y_b = _dot(ob_ref[...], wb_ref[...])
    merged = ga_ref[...].astype(F32) * y_a + gb_ref[...].astype(F32) * y_b
    h = _dot(merged.astype(BF16), wo_ref[...])
    x1 = _layer_norm(alpha * x_ref[...] + h, g_ref[...], b_ref[...])
    x1_ref[...] = x1

    x_hi = x1.astype(BF16)
    x_lo = (x1 - x_hi.astype(F32)).astype(BF16)
    wr_hi = wr_hi_ref[...]
    logits = _dot(x_hi, wr_hi) + _dot(x_lo, wr_hi) + _dot(x_hi, wr_lo_ref[...]) + br_ref[...]
    lane = lax.broadcasted_iota(jnp.int32, logits.shape, 1)
    neg = -jnp.inf
    g_logits = jnp.where(lane < N_GROUPS, logits, neg)
    g_max, grp = _first_index_of_max(g_logits, lane)
    p_grp = 1.0 / jnp.sum(jnp.exp(g_logits - g_max), axis=-1, keepdims=True)
    lo = N_GROUPS + EXPERTS_PER_GROUP * grp
    e_logits = jnp.where((lane >= lo) & (lane < lo + EXPERTS_PER_GROUP), logits, neg)
    v1, i1 = _first_index_of_max(e_logits, lane)
    v2, i2 = _first_index_of_max(jnp.where(lane == i1, neg, e_logits), lane)
    e2 = jnp.exp(v2 - v1)
    w1 = 1.0 / (1.0 + e2)
    gate_ref[...] = jnp.where(lane == 0, w1 * p_grp, jnp.where(lane == 1, e2 * w1 * p_grp, 0.0))
    idx_ref[...] = jnp.where(lane == 0, i1 - N_GROUPS, jnp.where(lane == 1, i2 - N_GROUPS, 0))


def _merge(alpha, o_a, o_b, ga, gb, x, wa, wb, wo, ln_g, ln_b, wr_hi, wr_lo, br):
    t, d = x.shape
    tm = min(MERGE_TM, t)
    row = lambda n: pl.BlockSpec((tm, n), lambda i: (i, 0))
    full = lambda a: pl.BlockSpec(a.shape, lambda i: (0, 0))
    return pl.pallas_call(
        functools.partial(_merge_kernel, alpha),
        grid=(t // tm,),
        in_specs=[row(d), row(d), row(d), row(d), row(d), full(wa), full(wb), full(wo), full(ln_g), full(ln_b),
                  full(wr_hi), full(wr_lo), full(br)],
        out_specs=[row(d), row(LANES), row(LANES)],
        out_shape=[jax.ShapeDtypeStruct((t, d), F32), jax.ShapeDtypeStruct((t, LANES), F32),
                   jax.ShapeDtypeStruct((t, LANES), jnp.int32)],
        compiler_params=_cparams(("parallel",)),
        name="merge_ln_router",
    )(o_a, o_b, ga, gb, x, wa, wb, wo, ln_g, ln_b, wr_hi, wr_lo, br)


def _moe_kernel(blk_e_ref, n_used_ref, src_ref, src_next_ref, dst_ref, x_hbm, wg_ref, wu_ref, wd_ref, y_hbm,
                xbuf, ybuf, wg_bf, wu_bf, wd_bf, gsem, ssem):
    i = pl.program_id(0)
    n_used = n_used_ref[0]
    blk = xbuf.shape[1]
    slot = i % 2
    other = 1 - slot

    def start_gather(idx_ref, s):
        def body(r, carry):
            pltpu.make_async_copy(x_hbm.at[pl.ds(idx_ref[0, r], 1), :], xbuf.at[s, pl.ds(r, 1), :],
                                  gsem.at[s]).start()
            return carry
        lax.fori_loop(0, blk, body, 0, unroll=8)

    def start_scatter(s):
        def body(r, carry):
            pltpu.make_async_copy(ybuf.at[s, pl.ds(r, 1), :], y_hbm.at[pl.ds(dst_ref[0, r], 1), :],
                                  ssem.at[s]).start()
            return carry
        lax.fori_loop(0, blk, body, 0, unroll=8)

    def wait_gather(s):
        pltpu.make_async_copy(x_hbm.at[pl.ds(0, blk), :], xbuf.at[s], gsem.at[s]).wait()

    def wait_scatter(s):
        pltpu.make_async_copy(ybuf.at[s], y_hbm.at[pl.ds(0, blk), :], ssem.at[s]).wait()

    @pl.when(i == 0)
    def _():
        start_gather(src_ref, 0)

    @pl.when(i + 1 < n_used)
    def _():
        start_gather(src_next_ref, other)

    @pl.when(i < n_used)
    def _():
        @pl.when((i == 0) | (blk_e_ref[i] != blk_e_ref[jnp.maximum(i - 1, 0)]))
        def _():
            wg_bf[...] = wg_ref[...].astype(BF16)
            wu_bf[...] = wu_ref[...].astype(BF16)
            wd_bf[...] = wd_ref[...].astype(BF16)

        wait_gather(slot)
        xb = xbuf[slot].astype(BF16)
        gate = _dot(xb, wg_bf[...])
        up = _dot(xb, wu_bf[...])
        hidden = (gate * jax.nn.sigmoid(gate) * up).astype(BF16)
        ybuf[slot] = _dot(hidden, wd_bf[...])

        @pl.when(i >= 1)
        def _():
            wait_scatter(other)

        start_scatter(slot)

        @pl.when(i == n_used - 1)
        def _():
            wait_scatter(slot)

    @pl.when(i >= n_used)
    def _():
        ybuf[slot] = jnp.zeros((blk, ybuf.shape[2]), F32)
        fill = pltpu.make_async_copy(ybuf.at[slot], y_hbm.at[pl.ds(pl.multiple_of(i * blk, blk), blk), :],
                                     ssem.at[slot])
        fill.start()
        fill.wait()


def _moe(x1, blk_e, n_used, src, dst, wg, wu, wd, layer):
    t, d = x1.shape
    n_blocks = blk_e.shape[0]
    blk = src.shape[-1]
    de = wg.shape[-1]
    smem_idx = lambda fn: pl.BlockSpec((None, 1, blk), fn, memory_space=pltpu.SMEM)
    cur = lambda i, be, nu: (i, 0, 0)
    nxt = lambda i, be, nu: (jnp.minimum(i + 1, n_blocks - 1), 0, 0)
    expert = lambda i, be, nu: (layer, be[i], 0, 0)
    grid_spec = pltpu.PrefetchScalarGridSpec(
        num_scalar_prefetch=2,
        grid=(n_blocks,),
        in_specs=[smem_idx(cur), smem_idx(nxt), smem_idx(cur),
                  pl.BlockSpec(memory_space=pl.ANY),
                  pl.BlockSpec((None, None, d, de), expert),
                  pl.BlockSpec((None, None, d, de), expert),
                  pl.BlockSpec((None, None, de, d), expert)],
        out_specs=pl.BlockSpec(memory_space=pl.ANY),
        scratch_shapes=[pltpu.VMEM((2, blk, d), F32), pltpu.VMEM((2, blk, d), F32),
                        pltpu.VMEM((d, de), BF16), pltpu.VMEM((d, de), BF16), pltpu.VMEM((de, d), BF16),
                        pltpu.SemaphoreType.DMA((2,)), pltpu.SemaphoreType.DMA((2,))],
    )
    return pl.pallas_call(
        _moe_kernel,
        grid_spec=grid_spec,
        out_shape=jax.ShapeDtypeStruct((n_blocks * blk, d), F32),
        compiler_params=_cparams(("arbitrary",)),
        name="moe_experts",
    )(blk_e, n_used, src, src, dst, x1, wg, wu, wd)


def _moe_plan(expert_idx, t):
    a = t * TOP_K
    blk = min(MOE_BLK, a)
    n_blocks = a // blk + N_EXPERTS
    n_slots = n_blocks * blk
    flat_e = expert_idx.reshape(a)
    order = jnp.argsort(flat_e).astype(jnp.int32)
    counts = jnp.bincount(flat_e, length=N_EXPERTS).astype(jnp.int32)
    padded = (counts + blk - 1) // blk * blk
    pad_end = jnp.cumsum(padded)
    pad_start = pad_end - padded
    start = jnp.cumsum(counts) - counts
    n_used = (pad_end[-1] // blk).astype(jnp.int32)
    block_start = jnp.arange(n_blocks, dtype=jnp.int32) * blk
    blk_e = jnp.sum(block_start[:, None] >= pad_end[None, :], axis=1).astype(jnp.int32)
    blk_e = jnp.minimum(blk_e, N_EXPERTS - 1)
    blk_e = jnp.where(jnp.arange(n_blocks) < n_used, blk_e, blk_e[jnp.maximum(n_used - 1, 0)])
    slot = jnp.arange(n_slots, dtype=jnp.int32)
    slot_e = jnp.repeat(blk_e, blk)
    r = slot - pad_start[slot_e]
    valid = (r < counts[slot_e]) & (slot < pad_end[-1])
    assign = order[jnp.clip(start[slot_e] + r, 0, a - 1)]
    src = jnp.where(valid, assign // TOP_K, 0)
    pad_rank = jnp.cumsum(jnp.where(valid, 0, 1)) - 1
    dst = jnp.where(valid, (assign % TOP_K) * t + assign // TOP_K,
                    jnp.where(slot >= pad_end[-1], slot, a + pad_rank))
    return (blk_e, n_used.reshape(1), src.reshape(n_blocks, 1, blk).astype(jnp.int32),
            dst.reshape(n_blocks, 1, blk).astype(jnp.int32))


def _combine_kernel(alpha, x_ref, y0_ref, y1_ref, gate_ref, g_ref, b_ref, o_ref, obf_ref):
    gate = gate_ref[...]
    f = gate[:, 0:1] * y0_ref[...] + gate[:, 1:2] * y1_ref[...]
    out = _layer_norm(alpha * x_ref[...] + f, g_ref[...], b_ref[...])
    o_ref[...] = out
    obf_ref[...] = out.astype(BF16)


def _combine(alpha, x1, y, gate, ln_g, ln_b):
    t, d = x1.shape
    tm = min(COMB_TM, t)
    nt = t // tm
    row = lambda n: pl.BlockSpec((tm, n), lambda i: (i, 0))
    full = lambda a: pl.BlockSpec(a.shape, lambda i: (0, 0))
    return pl.pallas_call(
        functools.partial(_combine_kernel, alpha),
        grid=(nt,),
        in_specs=[row(d), row(d), pl.BlockSpec((tm, d), lambda i: (nt + i, 0)), row(LANES), full(ln_g), full(ln_b)],
        out_specs=[row(d), row(d)],
        out_shape=[jax.ShapeDtypeStruct((t, d), F32), jax.ShapeDtypeStruct((t, d), BF16)],
        compiler_params=_cparams(("parallel",)),
        name="combine_ln",
    )(x1, y, y, gate, ln_g, ln_b)


def kernel(x, w_in, b_fox_f, fox_q_gain, fox_k_gain, hg_lb_logits, hg_norm_gain, w_br_a, w_br_b, w_out, ln1_g, ln1_b, router_g_w, router_g_b, router_e_w, router_e_b, w_gate, w_up, w_down, ln2_g, ln2_b):
    batch, seq, d = x.shape
    depth = w_in.shape[0]
    t = batch * seq
    heads = d // HEAD_DIM
    alpha = (2 * depth) ** 0.25
    assert w_in.shape[2] == 9 * d + FOX_F_COLS

    lb_p = jax.nn.softmax(hg_lb_logits.astype(F32), axis=0)
    lb_all = jnp.cumsum(lb_p, axis=0) - lb_p[0]
    fox_scale = HEAD_DIM ** -0.5 * LOG2E

    xf = x.reshape(t, d)
    xb = xf.astype(BF16)
    for l in range(depth):
        w = w_in[l].astype(BF16)
        col = lambda j: w[:, j * d:(j + 1) * d]
        w_gates = w[:, 7 * d + FOX_F_COLS:]
        w_ff = jnp.pad(w[:, 7 * d:7 * d + FOX_F_COLS], ((0, 0), (0, LANES - FOX_F_COLS)))
        b_ff = jnp.pad(b_fox_f[l], (0, LANES - FOX_F_COLS)).reshape(1, LANES)

        (q_a,) = _proj(xb, col(0), "silu")
        logf, k_a = _proj(xb, col(1), "forget", aux=(lb_all[l].reshape(1, d),), out_dtypes=(F32, BF16))
        (v_a,) = _proj(xb, col(2), "plain")
        (hg_act,) = _proj(xb, col(3), "silu")
        (q_b,) = _proj(xb, col(4), "rms", aux=(jnp.tile(fox_q_gain[l] * fox_scale, heads).reshape(1, d),))
        (k_b,) = _proj(xb, col(5), "rms", aux=(jnp.tile(fox_k_gain[l], heads).reshape(1, d),))
        (v_b,) = _proj(xb, col(6), "plain")
        (g_a,) = _proj(xb, w_gates[:, :d], "sigmoid")
        (g_b,) = _proj(xb, w_gates[:, d:], "sigmoid")
        nc_rep = _fox_forget(xb, w_ff, b_ff, seq)

        o_a = _hgrn(q_a, k_a, v_a, logf, hg_act, hg_norm_gain[l].reshape(1, HEAD_DIM), batch, seq)

        o_b = _fox(q_b, k_b, v_b, nc_rep, batch, seq)

        w_r = jnp.pad(jnp.concatenate([router_g_w[l], router_e_w[l]], axis=1),
                      ((0, 0), (0, LANES - N_GROUPS - N_EXPERTS)))
        b_r = jnp.pad(jnp.concatenate([router_g_b[l], router_e_b[l]]), (0, LANES - N_GROUPS - N_EXPERTS)).reshape(1, LANES)
        w_r_hi = w_r.astype(BF16)
        w_r_lo = (w_r - w_r_hi.astype(F32)).astype(BF16)
        x1, gate, idx = _merge(alpha, o_a, o_b, g_a, g_b, xf, w_br_a[l].astype(BF16), w_br_b[l].astype(BF16),
                               w_out[l].astype(BF16), ln1_g[l].reshape(1, d), ln1_b[l].reshape(1, d),
                               w_r_hi, w_r_lo, b_r)

        blk_e, n_used, src, dst = _moe_plan(idx[:, :TOP_K], t)
        y = _moe(x1, blk_e, n_used, src, dst, w_gate, w_up, w_down, l)
        xf, xb = _combine(alpha, x1, y, gate, ln2_g[l].reshape(1, d), ln2_b[l].reshape(1, d))
    return xf.reshape(batch, seq, d)
```

```python
import functools

import jax
import jax.numpy as jnp
from jax import lax
from jax.experimental import pallas as pl
from jax.experimental.pallas import tpu as pltpu

F32 = jnp.float32
BF16 = jnp.bfloat16

HEAD_DIM = 128
FOX_F_COLS = 8
N_GROUPS = 4
EXPERTS_PER_GROUP = 8
N_EXPERTS = N_GROUPS * EXPERTS_PER_GROUP
TOP_K = 2
LN_EPS = 1e-5
NORM_EPS = 1e-6
LOG2E = 1.4426950408889634
FOX_MASKED = -1e30

LANES = 128
SUBLANES = 8
VMEM_LIMIT_BYTES = 48 * 1024 * 1024

PROJ_TM = 1024
HG_CHUNK = 64
HG_SUB = 8
HG_TS = 256
FOX_TQ = 512
FOX_TK = 256
FOX_CW = 256
MERGE_TM = 512
MOE_BLK = 256
COMB_TM = 512


def _cparams(sem):
    return pltpu.CompilerParams(dimension_semantics=sem, vmem_limit_bytes=VMEM_LIMIT_BYTES)


def _split3(a):
    hi = a.astype(BF16)
    r1 = a - hi.astype(F32)
    mid = r1.astype(BF16)
    lo = (r1 - mid.astype(F32)).astype(BF16)
    return hi, mid, lo


def _dot(a, b):
    return jnp.dot(a, b, preferred_element_type=F32)


def _dot_nt(a, b):
    return lax.dot_general(a, b, (((1,), (1,)), ((), ())), preferred_element_type=F32)


def _dot_tn(a, b):
    return lax.dot_general(a, b, (((0,), (0,)), ((), ())), preferred_element_type=F32)


def _proj_kernel(kind, x_ref, w_ref, *refs):
    acc = _dot(x_ref[...], w_ref[...])
    if kind == "plain":
        (o_ref,) = refs
        o_ref[...] = acc.astype(o_ref.dtype)
    elif kind == "silu":
        (o_ref,) = refs
        o_ref[...] = (acc * jax.nn.sigmoid(acc)).astype(o_ref.dtype)
    elif kind == "sigmoid":
        (o_ref,) = refs
        o_ref[...] = jax.nn.sigmoid(acc).astype(o_ref.dtype)
    elif kind == "forget":
        lb_ref, logf_ref, k_ref = refs
        lb = lb_ref[...]
        f = lb + (1.0 - lb) * jax.nn.sigmoid(acc)
        logf_ref[...] = jnp.log(f)
        k_ref[...] = (1.0 - f).astype(k_ref.dtype)
    elif kind == "rms":
        gain_ref, o_ref = refs
        for h in range(acc.shape[1] // HEAD_DIM):
            sl = slice(h * HEAD_DIM, (h + 1) * HEAD_DIM)
            a = acc[:, sl]
            ms = jnp.mean(a * a, axis=-1, keepdims=True)
            o_ref[:, sl] = (a * lax.rsqrt(ms + NORM_EPS) * gain_ref[:, sl]).astype(o_ref.dtype)
    else:
        raise ValueError(kind)


def _proj(x_bf, w_bf, kind, aux=(), out_dtypes=(BF16,)):
    t, d = x_bf.shape
    n = w_bf.shape[1]
    tm = min(PROJ_TM, t)
    in_specs = [pl.BlockSpec((tm, d), lambda i: (i, 0)), pl.BlockSpec((d, n), lambda i: (0, 0))]
    in_specs += [pl.BlockSpec((1, n), lambda i: (0, 0)) for _ in aux]
    out_specs = [pl.BlockSpec((tm, n), lambda i: (i, 0)) for _ in out_dtypes]
    out_shape = [jax.ShapeDtypeStruct((t, n), dt) for dt in out_dtypes]
    outs = pl.pallas_call(
        functools.partial(_proj_kernel, kind),
        grid=(t // tm,),
        in_specs=in_specs,
        out_specs=out_specs,
        out_shape=out_shape,
        compiler_params=_cparams(("parallel",)),
        name=f"proj_{kind}",
    )(x_bf, w_bf, *aux)
    return outs


def _fox_forget_kernel(tiles_per_seq, x_ref, w_ref, b_ref, nc_ref, carry_ref):
    i = pl.program_id(0)

    @pl.when(i % tiles_per_seq == 0)
    def _():
        carry_ref[...] = jnp.zeros_like(carry_ref)

    z = _dot(x_ref[...], w_ref[...]) + b_ref[...]
    lf = jnp.minimum(z, 0.0) - jnp.log(1.0 + jnp.exp(-jnp.abs(z)))
    tm = lf.shape[0]
    row = lax.broadcasted_iota(jnp.int32, (tm, tm), 0)
    col = lax.broadcasted_iota(jnp.int32, (tm, tm), 1)
    tri = jnp.where(col <= row, 1.0, 0.0).astype(BF16)
    hi, mid, lo = _split3(lf)
    c = _dot(tri, hi) + _dot(tri, mid) + _dot(tri, lo) + carry_ref[...]
    carry_ref[...] = c[tm - 1:tm, :]
    for h in range(nc_ref.shape[1] // HEAD_DIM):
        nc_ref[:, h * HEAD_DIM:(h + 1) * HEAD_DIM] = jnp.broadcast_to(c[:, h:h + 1] * (-LOG2E), (tm, HEAD_DIM))


def _fox_forget(x_bf, w_pad_bf, b_pad, seq):
    t, d = x_bf.shape
    tm = min(512, seq)
    return pl.pallas_call(
        functools.partial(_fox_forget_kernel, seq // tm),
        grid=(t // tm,),
        in_specs=[pl.BlockSpec((tm, d), lambda i: (i, 0)),
                  pl.BlockSpec((d, LANES), lambda i: (0, 0)),
                  pl.BlockSpec((1, LANES), lambda i: (0, 0))],
        out_specs=pl.BlockSpec((tm, FOX_F_COLS * HEAD_DIM), lambda i: (i, 0)),
        out_shape=jax.ShapeDtypeStruct((t, FOX_F_COLS * HEAD_DIM), F32),
        scratch_shapes=[pltpu.VMEM((1, LANES), F32)],
        compiler_params=_cparams(("arbitrary",)),
        name="fox_forget_cumsum",
    )(x_bf, w_pad_bf, b_pad)


def _bcast_row(a, period, row):
    n, w = a.shape
    a3 = a.reshape(n // period, period, w)
    return jnp.broadcast_to(a3[:, row:row + 1, :], a3.shape).reshape(n, w)


def _hgrn_kernel(q_ref, k_ref, v_ref, g_ref, hg_ref, gain_ref, tri_ref, u_ref, o_ref, st_ref):
    ts = q_ref.shape[0]
    n_chunks = ts // HG_CHUNK

    @pl.when(pl.program_id(2) == 0)
    def _():
        st_ref[...] = jnp.zeros_like(st_ref)

    q = q_ref[...].astype(F32)
    k = k_ref[...].astype(F32)
    v = v_ref[...]
    g = g_ref[...] * LOG2E
    g_hi = g.astype(BF16)
    g_lo = (g - g_hi.astype(F32)).astype(BF16)
    tri = tri_ref[...]
    b = _dot(tri, g_hi) + _dot(tri, g_lo)

    row128 = lax.broadcasted_iota(jnp.int32, (ts, HEAD_DIM), 0)
    row64 = lax.broadcasted_iota(jnp.int32, (ts, HG_CHUNK), 0) % HG_CHUNK
    col64 = lax.broadcasted_iota(jnp.int32, (ts, HG_CHUNK), 1)

    a_key = b - jnp.log2(k)
    parts = []
    for s in range(HG_SUB):
        a_s = _bcast_row(a_key, HG_SUB, s)
        parts.append((q * jnp.exp2(jnp.minimum(b - a_s, 0.0))).astype(BF16))
    scores = _dot(jnp.concatenate(parts, axis=1), u_ref[...])
    same_sub = (row64 // HG_SUB) == (col64 // HG_SUB)
    scores = jnp.where(same_sub & ((col64 % HG_SUB) <= (row64 % HG_SUB)), scores, 0.0)

    level_q, level_k, level_mask = [], [], []
    m = HG_SUB
    while m < HG_CHUNK:
        ref_b = _bcast_row(b, 2 * m, m - 1)
        second = (row128 % (2 * m)) >= m
        e = jnp.exp2(jnp.where(second, b - ref_b, ref_b - b))
        level_q.append((q * e).astype(BF16))
        level_k.append((k * e).astype(BF16))
        level_mask.append(((row64 // (2 * m)) == (col64 // (2 * m)))
                          & ((row64 % (2 * m)) >= m) & ((col64 % (2 * m)) < m))
        m *= 2

    b_last = _bcast_row(b, HG_CHUNK, HG_CHUNK - 1)
    q_in = (q * jnp.exp2(b)).astype(BF16)
    k_out = (k * jnp.exp2(b_last - b)).astype(BF16)

    chunks = [slice(c * HG_CHUNK, (c + 1) * HG_CHUNK) for c in range(n_chunks)]
    intra, state_in = [], []
    for sl in chunks:
        a = scores[sl]
        for lq, lk, lm in zip(level_q, level_k, level_mask):
            a = a + jnp.where(lm[sl], _dot_nt(lq[sl], lk[sl]), 0.0)
        intra.append(_dot(a.astype(BF16), v[sl]))
        state_in.append(_dot_tn(v[sl], k_out[sl]))

    gain = gain_ref[...]
    st = st_ref[...]
    for c, sl in enumerate(chunks):
        o = intra[c] + _dot_nt(q_in[sl], st.astype(BF16))
        st = st * jnp.exp2(b[sl.stop - 1:sl.stop, :]) + state_in[c]
        ms = jnp.mean(o * o, axis=-1, keepdims=True)
        o = o * lax.rsqrt(ms + NORM_EPS) * gain * hg_ref[sl, :].astype(F32)
        o_ref[sl, :] = o.astype(o_ref.dtype)
    st_ref[...] = st


def _hgrn(q, k, v, logf, hg_act, gain, batch, seq):
    d = q.shape[-1]
    heads = d // HEAD_DIM
    ts = min(HG_TS, seq)
    shp = (batch, seq, d)
    q, k, v, logf, hg_act = (a.reshape(shp) for a in (q, k, v, logf, hg_act))
    r = jnp.arange(ts)
    tri = ((r[:, None] // HG_CHUNK == r[None, :] // HG_CHUNK) & (r[None, :] <= r[:, None])).astype(BF16)
    u_rows = jnp.arange(HG_SUB * HEAD_DIM) // HEAD_DIM
    u = (u_rows[:, None] == (jnp.arange(HG_CHUNK)[None, :] % HG_SUB)).astype(BF16)
    tile = pl.BlockSpec((None, ts, HEAD_DIM), lambda b, h, s: (b, s, h))
    out = pl.pallas_call(
        _hgrn_kernel,
        grid=(batch, heads, seq // ts),
        in_specs=[tile, tile, tile, tile, tile,
                  pl.BlockSpec((1, HEAD_DIM), lambda b, h, s: (0, 0)),
                  pl.BlockSpec((ts, ts), lambda b, h, s: (0, 0)),
                  pl.BlockSpec((HG_SUB * HEAD_DIM, HG_CHUNK), lambda b, h, s: (0, 0))],
        out_specs=tile,
        out_shape=jax.ShapeDtypeStruct(shp, BF16),
        scratch_shapes=[pltpu.VMEM((HEAD_DIM, HEAD_DIM), F32)],
        compiler_params=_cparams(("parallel", "parallel", "arbitrary")),
        name="hgrn2",
    )(q, k, v, logf, hg_act, gain, tri, u)
    return out.reshape(batch * seq, d)


def _fox_kernel(qt_ref, k_ref, vt_ref, nc_ref, o_ref, acc_ref, sc_ref, pr_ref):
    tq = qt_ref.shape[1]
    tk = vt_ref.shape[2]
    n_streams = acc_ref.shape[0]
    qi = pl.program_id(2)
    cw = min(FOX_CW, tq)
    n_cg = tq // cw
    acc_ref[...] = jnp.zeros_like(acc_ref)

    chains = [(s, cg) for s in range(n_streams) for cg in range(n_cg)]

    def block_starts(g):
        return [pl.multiple_of((g * n_streams + s) * tk, tk) for s in range(n_streams)]

    def score_matmuls(g):
        starts = block_starts(g)
        for ci, (s, cg) in enumerate(chains):
            sc_ref[g % 2, ci] = _dot(k_ref[pl.ds(starts[s], tk), :], qt_ref[:, cg * cw:(cg + 1) * cw])

    def softmax(g, carry, masked):
        starts = block_starts(g)
        ml_new, alphas = [], []
        for ci, ((s, cg), (m_old, l_old)) in enumerate(zip(chains, carry)):
            if masked and s * tk >= (cg + 1) * cw:
                ml_new.append((m_old, l_old))
                alphas.append(jnp.ones((1, cw), F32))
                pr_ref[ci] = jnp.zeros((tk, cw), BF16)
                continue
            st = sc_ref[g % 2, ci]
            st = st + jnp.concatenate([nc_ref[pl.ds(starts[s], tk), :]] * (cw // HEAD_DIM), axis=1)
            if masked:
                key_pos = starts[s] + lax.broadcasted_iota(jnp.int32, st.shape, 0)
                query_pos = qi * tq + cg * cw + lax.broadcasted_iota(jnp.int32, st.shape, 1)
                st = jnp.where(key_pos <= query_pos, st, FOX_MASKED)
            m_new = jnp.maximum(m_old, jnp.max(st, axis=0, keepdims=True))
            alpha = jnp.exp2(m_old - m_new)
            p = jnp.exp2(st - m_new)
            ml_new.append((m_new, alpha * l_old + jnp.sum(p, axis=0, keepdims=True)))
            alphas.append(alpha)
            pr_ref[ci] = p.astype(BF16)
        return tuple(ml_new), tuple(alphas)

    def value_matmuls(g, alphas):
        for ci, (s, cg) in enumerate(chains):
            cols = slice(cg * cw, (cg + 1) * cw)
            acc_ref[s, :, cols] = acc_ref[s, :, cols] * alphas[ci] + _dot(vt_ref[g * n_streams + s], pr_ref[ci])

    init = ((jnp.full((1, cw), FOX_MASKED, F32), jnp.zeros((1, cw), F32)),) * len(chains)
    ones = (jnp.ones((1, cw), F32),) * len(chains)

    def body(g, state):
        carry, alphas = state
        value_matmuls(jnp.maximum(g - 1, 0), alphas)
        state = softmax(g, carry, False)
        score_matmuls(g + 1)
        return state

    pr_ref[...] = jnp.zeros_like(pr_ref)
    score_matmuls(0)
    carry, alphas = lax.fori_loop(0, qi, body, (init, ones))
    value_matmuls(jnp.maximum(qi - 1, 0), alphas)
    carry, alphas = softmax(qi, carry, True)
    value_matmuls(qi, alphas)
    carry = tuple(tuple(carry[s * n_cg + cg] for cg in range(n_cg)) for s in range(n_streams))

    for cg in range(n_cg):
        cols = slice(cg * cw, (cg + 1) * cw)
        m = carry[0][cg][0]
        for s in range(1, n_streams):
            m = jnp.maximum(m, carry[s][cg][0])
        l = jnp.zeros((1, cw), F32)
        acc = jnp.zeros((HEAD_DIM, cw), F32)
        for s in range(n_streams):
            w = jnp.exp2(carry[s][cg][0] - m)
            l = l + w * carry[s][cg][1]
            acc = acc + w * acc_ref[s, :, cols]
        o_ref[cols, :] = (acc * (1.0 / l)).T.astype(o_ref.dtype)


def _fox(q, k, v, nc_rep, batch, seq):
    d = q.shape[-1]
    heads = d // HEAD_DIM
    tq = min(FOX_TQ, seq)
    tk = min(FOX_TK, seq)
    shp = (batch, seq, d)
    qt = q.reshape(batch, seq, heads, HEAD_DIM).transpose(0, 2, 3, 1)
    vt = v.reshape(batch, seq // tk, tk, heads, HEAD_DIM).transpose(0, 3, 1, 4, 2)
    kvfull = pl.BlockSpec((None, seq, HEAD_DIM), lambda b, h, i: (b, 0, h))
    out = pl.pallas_call(
        _fox_kernel,
        grid=(batch, heads, seq // tq),
        in_specs=[pl.BlockSpec((None, None, HEAD_DIM, tq), lambda b, h, i: (b, h, 0, i)),
                  kvfull,
                  pl.BlockSpec((None, None, seq // tk, HEAD_DIM, tk), lambda b, h, i: (b, h, 0, 0, 0)),
                  kvfull],
        out_specs=pl.BlockSpec((None, tq, HEAD_DIM), lambda b, h, i: (b, i, h)),
        out_shape=jax.ShapeDtypeStruct(shp, BF16),
        scratch_shapes=[pltpu.VMEM((tq // tk, HEAD_DIM, tq), F32),
                        pltpu.VMEM((2, (tq // tk) * (tq // min(FOX_CW, tq)), tk, min(FOX_CW, tq)), F32),
                        pltpu.VMEM(((tq // tk) * (tq // min(FOX_CW, tq)), tk, min(FOX_CW, tq)), BF16)],
        compiler_params=_cparams(("parallel", "parallel", "arbitrary")),
        name="fox_attention",
    )(qt, k.reshape(shp), vt, nc_rep.reshape(shp))
    return out.reshape(batch * seq, d)


def _layer_norm(z, g, b):
    mu = jnp.mean(z, axis=-1, keepdims=True)
    zc = z - mu
    var = jnp.mean(zc * zc, axis=-1, keepdims=True)
    return zc * lax.rsqrt(var + LN_EPS) * g + b


def _first_index_of_max(vals, lane):
    vmax = jnp.max(vals, axis=-1, keepdims=True)
    idx = jnp.min(jnp.where(vals == vmax, lane, LANES), axis=-1, keepdims=True)
    return vmax, idx


def _merge_kernel(alpha, oa_ref, ob_ref, ga_ref, gb_ref, x_ref, wa_ref, wb_ref, wo_ref, g_ref, b_ref,
                  wr_hi_ref, wr_lo_ref, br_ref, x1_ref, gate_ref, idx_ref):
    y_a = _dot(oa_ref[...], wa_ref[...])
    y_b = _dot(ob_ref[...], wb_ref[...])
    merged = ga_ref[...].astype(F32) * y_a + gb_ref[...].astype(F32) * y_b
    h = _dot(merged.astype(BF16), wo_ref[...])
    x1 = _layer_norm(alpha * x_ref[...] + h, g_ref[...], b_ref[...])
    x1_ref[...] = x1

    x_hi = x1.astype(BF16)
    x_lo = (x1 - x_hi.astype(F32)).astype(BF16)
    wr_hi = wr_hi_ref[...]
    logits = _dot(x_hi, wr_hi) + _dot(x_lo, wr_hi) + _dot(x_hi, wr_lo_ref[...]) + br_ref[...]
    lane = lax.broadcasted_iota(jnp.int32, logits.shape, 1)
    neg = -jnp.inf
    g_logits = jnp.where(lane < N_GROUPS, logits, neg)
    g_max, grp = _first_index_of_max(g_logits, lane)
    p_grp = 1.0 / jnp.sum(jnp.exp(g_logits - g_max), axis=-1, keepdims=True)
    lo = N_GROUPS + EXPERTS_PER_GROUP * grp
    e_logits = jnp.where((lane >= lo) & (lane < lo + EXPERTS_PER_GROUP), logits, neg)
    v1, i1 = _first_index_of_max(e_logits, lane)
    v2, i2 = _first_index_of_max(jnp.where(lane == i1, neg, e_logits), lane)
    e2 = jnp.exp(v2 - v1)
    w1 = 1.0 / (1.0 + e2)
    gate_ref[...] = jnp.where(lane == 0, w1 * p_grp, jnp.where(lane == 1, e2 * w1 * p_grp, 0.0))
    idx_ref[...] = jnp.where(lane == 0, i1 - N_GROUPS, jnp.where(lane == 1, i2 - N_GROUPS, 0))


def _merge(alpha, o_a, o_b, ga, gb, x, wa, wb, wo, ln_g, ln_b, wr_hi, wr_lo, br):
    t, d = x.shape
    tm = min(MERGE_TM, t)
    row = lambda n: pl.BlockSpec((tm, n), lambda i: (i, 0))
    full = lambda a: pl.BlockSpec(a.shape, lambda i: (0, 0))
    return pl.pallas_call(
        functools.partial(_merge_kernel, alpha),
        grid=(t // tm,),
        in_specs=[row(d), row(d), row(d), row(d), row(d), full(wa), full(wb), full(wo), full(ln_g), full(ln_b),
                  full(wr_hi), full(wr_lo), full(br)],
        out_specs=[row(d), row(LANES), row(LANES)],
        out_shape=[jax.ShapeDtypeStruct((t, d), F32), jax.ShapeDtypeStruct((t, LANES), F32),
                   jax.ShapeDtypeStruct((t, LANES), jnp.int32)],
        compiler_params=_cparams(("parallel",)),
        name="merge_ln_router",
    )(o_a, o_b, ga, gb, x, wa, wb, wo, ln_g, ln_b, wr_hi, wr_lo, br)


def _moe_kernel(blk_e_ref, n_used_ref, src_ref, src_next_ref, dst_ref, x_hbm, wg_ref, wu_ref, wd_ref, y_hbm,
                xbuf, ybuf, wg_bf, wu_bf, wd_bf, gsem, ssem):
    i = pl.program_id(0)
    n_used = n_used_ref[0]
    blk = xbuf.shape[1]
    slot = i % 2
    other = 1 - slot

    def start_gather(idx_ref, s):
        def body(r, carry):
            pltpu.make_async_copy(x_hbm.at[pl.ds(idx_ref[0, r], 1), :], xbuf.at[s, pl.ds(r, 1), :],
                                  gsem.at[s]).start()
            return carry
        lax.fori_loop(0, blk, body, 0, unroll=8)

    def start_scatter(s):
        def body(r, carry):
            pltpu.make_async_copy(ybuf.at[s, pl.ds(r, 1), :], y_hbm.at[pl.ds(dst_ref[0, r], 1), :],
                                  ssem.at[s]).start()
            return carry
        lax.fori_loop(0, blk, body, 0, unroll=8)

    def wait_gather(s):
        pltpu.make_async_copy(x_hbm.at[pl.ds(0, blk), :], xbuf.at[s], gsem.at[s]).wait()

    def wait_scatter(s):
        pltpu.make_async_copy(ybuf.at[s], y_hbm.at[pl.ds(0, blk), :], ssem.at[s]).wait()

    @pl.when(i == 0)
    def _():
        start_gather(src_ref, 0)

    @pl.when(i + 1 < n_used)
    def _():
        start_gather(src_next_ref, other)

    @pl.when(i < n_used)
    def _():
        @pl.when((i == 0) | (blk_e_ref[i] != blk_e_ref[jnp.maximum(i - 1, 0)]))
        def _():
            wg_bf[...] = wg_ref[...].astype(BF16)
            wu_bf[...] = wu_ref[...].astype(BF16)
            wd_bf[...] = wd_ref[...].astype(BF16)

        wait_gather(slot)
        xb = xbuf[slot].astype(BF16)
        gate = _dot(xb, wg_bf[...])
        up = _dot(xb, wu_bf[...])
        hidden = (gate * jax.nn.sigmoid(gate) * up).astype(BF16)
        ybuf[slot] = _dot(hidden, wd_bf[...])

        @pl.when(i >= 1)
        def _():
            wait_scatter(other)

        start_scatter(slot)

        @pl.when(i == n_used - 1)
        def _():
            wait_scatter(slot)

    @pl.when(i >= n_used)
    def _():
        ybuf[slot] = jnp.zeros((blk, ybuf.shape[2]), F32)
        fill = pltpu.make_async_copy(ybuf.at[slot], y_hbm.at[pl.ds(pl.multiple_of(i * blk, blk), blk), :],
                                     ssem.at[slot])
        fill.start()
        fill.wait()


def _moe(x1, blk_e, n_used, src, dst, wg, wu, wd, layer):
    t, d = x1.shape
    n_blocks = blk_e.shape[0]
    blk = src.shape[-1]
    de = wg.shape[-1]
    smem_idx = lambda fn: pl.BlockSpec((None, 1, blk), fn, memory_space=pltpu.SMEM)
    cur = lambda i, be, nu: (i, 0, 0)
    nxt = lambda i, be, nu: (jnp.minimum(i + 1, n_blocks - 1), 0, 0)
    expert = lambda i, be, nu: (layer, be[i], 0, 0)
    grid_spec = pltpu.PrefetchScalarGridSpec(
        num_scalar_prefetch=2,
        grid=(n_blocks,),
        in_specs=[smem_idx(cur), smem_idx(nxt), smem_idx(cur),
                  pl.BlockSpec(memory_space=pl.ANY),
                  pl.BlockSpec((None, None, d, de), expert),
                  pl.BlockSpec((None, None, d, de), expert),
                  pl.BlockSpec((None, None, de, d), expert)],
        out_specs=pl.BlockSpec(memory_space=pl.ANY),
        scratch_shapes=[pltpu.VMEM((2, blk, d), F32), pltpu.VMEM((2, blk, d), F32),
                        pltpu.VMEM((d, de), BF16), pltpu.VMEM((d, de), BF16), pltpu.VMEM((de, d), BF16),
                        pltpu.SemaphoreType.DMA((2,)), pltpu.SemaphoreType.DMA((2,))],
    )
    return pl.pallas_call(
        _moe_kernel,
        grid_spec=grid_spec,
        out_shape=jax.ShapeDtypeStruct((n_blocks * blk, d), F32),
        compiler_params=_cparams(("arbitrary",)),
        name="moe_experts",
    )(blk_e, n_used, src, src, dst, x1, wg, wu, wd)


def _moe_plan(expert_idx, t):
    a = t * TOP_K
    blk = min(MOE_BLK, a)
    n_blocks = a // blk + N_EXPERTS
    n_slots = n_blocks * blk
    flat_e = expert_idx.reshape(a)
    order = jnp.argsort(flat_e).astype(jnp.int32)
    counts = jnp.bincount(flat_e, length=N_EXPERTS).astype(jnp.int32)
    padded = (counts + blk - 1) // blk * blk
    pad_end = jnp.cumsum(padded)
    pad_start = pad_end - padded
    start = jnp.cumsum(counts) - counts
    n_used = (pad_end[-1] // blk).astype(jnp.int32)
    block_start = jnp.arange(n_blocks, dtype=jnp.int32) * blk
    blk_e = jnp.sum(block_start[:, None] >= pad_end[None, :], axis=1).astype(jnp.int32)
    blk_e = jnp.minimum(blk_e, N_EXPERTS - 1)
    blk_e = jnp.where(jnp.arange(n_blocks) < n_used, blk_e, blk_e[jnp.maximum(n_used - 1, 0)])
    slot = jnp.arange(n_slots, dtype=jnp.int32)
    slot_e = jnp.repeat(blk_e, blk)
    r = slot - pad_start[slot_e]
    valid = (r < counts[slot_e]) & (slot < pad_end[-1])
    assign = order[jnp.clip(start[slot_e] + r, 0, a - 1)]
    src = jnp.where(valid, assign // TOP_K, 0)
    pad_rank = jnp.cumsum(jnp.where(valid, 0, 1)) - 1
    dst = jnp.where(valid, (assign % TOP_K) * t + assign // TOP_K,
                    jnp.where(slot >= pad_end[-1], slot, a + pad_rank))
    return (blk_e, n_used.reshape(1), src.reshape(n_blocks, 1, blk).astype(jnp.int32),
            dst.reshape(n_blocks, 1, blk).astype(jnp.int32))


def _combine_kernel(alpha, x_ref, y0_ref, y1_ref, gate_ref, g_ref, b_ref, o_ref, obf_ref):
    gate = gate_ref[...]
    f = gate[:, 0:1] * y0_ref[...] + gate[:, 1:2] * y1_ref[...]
    out = _layer_norm(alpha * x_ref[...] + f, g_ref[...], b_ref[...])
    o_ref[...] = out
    obf_ref[...] = out.astype(BF16)


def _combine(alpha, x1, y, gate, ln_g, ln_b):
    t, d = x1.shape
    tm = min(COMB_TM, t)
    nt = t // tm
    row = lambda n: pl.BlockSpec((tm, n), lambda i: (i, 0))
    full = lambda a: pl.BlockSpec(a.shape, lambda i: (0, 0))
    return pl.pallas_call(
        functools.partial(_combine_kernel, alpha),
        grid=(nt,),
        in_specs=[row(d), row(d), pl.BlockSpec((tm, d), lambda i: (nt + i, 0)), row(LANES), full(ln_g), full(ln_b)],
        out_specs=[row(d), row(d)],
        out_shape=[jax.ShapeDtypeStruct((t, d), F32), jax.ShapeDtypeStruct((t, d), BF16)],
        compiler_params=_cparams(("parallel",)),
        name="combine_ln",
    )(x1, y, y, gate, ln_g, ln_b)


def kernel(x, w_in, b_fox_f, fox_q_gain, fox_k_gain, hg_lb_logits, hg_norm_gain, w_br_a, w_br_b, w_out, ln1_g, ln1_b, router_g_w, router_g_b, router_e_w, router_e_b, w_gate, w_up, w_down, ln2_g, ln2_b):
    batch, seq, d = x.shape
    depth = w_in.shape[0]
    t = batch * seq
    heads = d // HEAD_DIM
    alpha = (2 * depth) ** 0.25
    assert w_in.shape[2] == 9 * d + FOX_F_COLS

    lb_p = jax.nn.softmax(hg_lb_logits.astype(F32), axis=0)
    lb_all = jnp.cumsum(lb_p, axis=0) - lb_p[0]
    fox_scale = HEAD_DIM ** -0.5 * LOG2E

    xf = x.reshape(t, d)
    xb = xf.astype(BF16)
    for l in range(depth):
        w = w_in[l].astype(BF16)
        col = lambda j: w[:, j * d:(j + 1) * d]
        w_gates = w[:, 7 * d + FOX_F_COLS:]
        w_ff = jnp.pad(w[:, 7 * d:7 * d + FOX_F_COLS], ((0, 0), (0, LANES - FOX_F_COLS)))
        b_ff = jnp.pad(b_fox_f[l], (0, LANES - FOX_F_COLS)).reshape(1, LANES)

        (q_a,) = _proj(xb, col(0), "silu")
        logf, k_a = _proj(xb, col(1), "forget", aux=(lb_all[l].reshape(1, d),), out_dtypes=(F32, BF16))
        (v_a,) = _proj(xb, col(2), "plain")
        (hg_act,) = _proj(xb, col(3), "silu")
        (q_b,) = _proj(xb, col(4), "rms", aux=(jnp.tile(fox_q_gain[l] * fox_scale, heads).reshape(1, d),))
        (k_b,) = _proj(xb, col(5), "rms", aux=(jnp.tile(fox_k_gain[l], heads).reshape(1, d),))
        (v_b,) = _proj(xb, col(6), "plain")
        (g_a,) = _proj(xb, w_gates[:, :d], "sigmoid")
        (g_b,) = _proj(xb, w_gates[:, d:], "sigmoid")
        nc_rep = _fox_forget(xb, w_ff, b_ff, seq)

        o_a = _hgrn(q_a, k_a, v_a, logf, hg_act, hg_norm_gain[l].reshape(1, HEAD_DIM), batch, seq)

        o_b = _fox(q_b, k_b, v_b, nc_rep, batch, seq)

        w_r = jnp.pad(jnp.concatenate([router_g_w[l], router_e_w[l]], axis=1),
                      ((0, 0), (0, LANES - N_GROUPS - N_EXPERTS)))
        b_r = jnp.pad(jnp.concatenate([router_g_b[l], router_e_b[l]]), (0, LANES - N_GROUPS - N_EXPERTS)).reshape(1, LANES)
        w_r_hi = w_r.astype(BF16)
        w_r_lo = (w_r - w_r_hi.astype(F32)).astype(BF16)
        x1, gate, idx = _merge(alpha, o_a, o_b, g_a, g_b, xf, w_br_a[l].astype(BF16), w_br_b[l].astype(BF16),
                               w_out[l].astype(BF16), ln1_g[l].reshape(1, d), ln1_b[l].reshape(1, d),
                               w_r_hi, w_r_lo, b_r)

        blk_e, n_used, src, dst = _moe_plan(idx[:, :TOP_K], t)
        y = _moe(x1, blk_e, n_used, src, dst, w_gate, w_up, w_down, l)
        xf, xb = _combine(alpha, x1, y, gate, ln2_g[l].reshape(1, d), ln2_b[l].reshape(1, d))
    return xf.reshape(batch, seq, d)
```

```python
import functools

import jax
import jax.numpy as jnp
from jax import lax
from jax.experimental import pallas as pl
from jax.experimental.pallas import tpu as pltpu

F32 = jnp.float32
BF16 = jnp.bfloat16

HEAD_DIM = 128
FOX_F_COLS = 8
N_GROUPS = 4
EXPERTS_PER_GROUP = 8
N_EXPERTS = N_GROUPS * EXPERTS_PER_GROUP
TOP_K = 2
LN_EPS = 1e-5
NORM_EPS = 1e-6
LOG2E = 1.4426950408889634
FOX_MASKED = -1e30

LANES = 128
SUBLANES = 8
VMEM_LIMIT_BYTES = 48 * 1024 * 1024

PROJ_TM = 1024
HG_CHUNK = 64
HG_SUB = 8
HG_TS = 256
FOX_TQ = 512
FOX_TK = 256
FOX_CW = 256
MERGE_TM = 512
MOE_BLK = 256
COMB_TM = 512


def _cparams(sem):
    return pltpu.CompilerParams(dimension_semantics=sem, vmem_limit_bytes=VMEM_LIMIT_BYTES)


def _split3(a):
    hi = a.astype(BF16)
    r1 = a - hi.astype(F32)
    mid = r1.astype(BF16)
    lo = (r1 - mid.astype(F32)).astype(BF16)
    return hi, mid, lo


def _dot(a, b):
    return jnp.dot(a, b, preferred_element_type=F32)


def _dot_nt(a, b):
    return lax.dot_general(a, b, (((1,), (1,)), ((), ())), preferred_element_type=F32)


def _dot_tn(a, b):
    return lax.dot_general(a, b, (((0,), (0,)), ((), ())), preferred_element_type=F32)


def _proj_kernel(kind, x_ref, w_ref, *refs):
    acc = _dot(x_ref[...], w_ref[...])
    if kind == "plain":
        (o_ref,) = refs
        o_ref[...] = acc.astype(o_ref.dtype)
    elif kind == "silu":
        (o_ref,) = refs
        o_ref[...] = (acc * jax.nn.sigmoid(acc)).astype(o_ref.dtype)
    elif kind == "sigmoid":
        (o_ref,) = refs
        o_ref[...] = jax.nn.sigmoid(acc).astype(o_ref.dtype)
    elif kind == "forget":
        lb_ref, logf_ref, k_ref = refs
        lb = lb_ref[...]
        f = lb + (1.0 - lb) * jax.nn.sigmoid(acc)
        logf_ref[...] = jnp.log(f)
        k_ref[...] = (1.0 - f).astype(k_ref.dtype)
    elif kind == "rms":
        gain_ref, o_ref = refs
        for h in range(acc.shape[1] // HEAD_DIM):
            sl = slice(h * HEAD_DIM, (h + 1) * HEAD_DIM)
            a = acc[:, sl]
            ms = jnp.mean(a * a, axis=-1, keepdims=True)
            o_ref[:, sl] = (a * lax.rsqrt(ms + NORM_EPS) * gain_ref[:, sl]).astype(o_ref.dtype)
    else:
        raise ValueError(kind)


def _proj(x_bf, w_bf, kind, aux=(), out_dtypes=(BF16,)):
    t, d = x_bf.shape
    n = w_bf.shape[1]
    tm = min(PROJ_TM, t)
    in_specs = [pl.BlockSpec((tm, d), lambda i: (i, 0)), pl.BlockSpec((d, n), lambda i: (0, 0))]
    in_specs += [pl.BlockSpec((1, n), lambda i: (0, 0)) for _ in aux]
    out_specs = [pl.BlockSpec((tm, n), lambda i: (i, 0)) for _ in out_dtypes]
    out_shape = [jax.ShapeDtypeStruct((t, n), dt) for dt in out_dtypes]
    outs = pl.pallas_call(
        functools.partial(_proj_kernel, kind),
        grid=(t // tm,),
        in_specs=in_specs,
        out_specs=out_specs,
        out_shape=out_shape,
        compiler_params=_cparams(("parallel",)),
        name=f"proj_{kind}",
    )(x_bf, w_bf, *aux)
    return outs


def _fox_forget_kernel(tiles_per_seq, x_ref, w_ref, b_ref, nc_ref, carry_ref):
    i = pl.program_id(0)

    @pl.when(i % tiles_per_seq == 0)
    def _():
        carry_ref[...] = jnp.zeros_like(carry_ref)

    z = _dot(x_ref[...], w_ref[...]) + b_ref[...]
    lf = jnp.minimum(z, 0.0) - jnp.log(1.0 + jnp.exp(-jnp.abs(z)))
    tm = lf.shape[0]
    row = lax.broadcasted_iota(jnp.int32, (tm, tm), 0)
    col = lax.broadcasted_iota(jnp.int32, (tm, tm), 1)
    tri = jnp.where(col <= row, 1.0, 0.0).astype(BF16)
    hi, mid, lo = _split3(lf)
    c = _dot(tri, hi) + _dot(tri, mid) + _dot(tri, lo) + carry_ref[...]
    carry_ref[...] = c[tm - 1:tm, :]
    for h in range(nc_ref.shape[1] // HEAD_DIM):
        nc_ref[:, h * HEAD_DIM:(h + 1) * HEAD_DIM] = jnp.broadcast_to(c[:, h:h + 1] * (-LOG2E), (tm, HEAD_DIM))


def _fox_forget(x_bf, w_pad_bf, b_pad, seq):
    t, d = x_bf.shape
    tm = min(512, seq)
    return pl.pallas_call(
        functools.partial(_fox_forget_kernel, seq // tm),
        grid=(t // tm,),
        in_specs=[pl.BlockSpec((tm, d), lambda i: (i, 0)),
                  pl.BlockSpec((d, LANES), lambda i: (0, 0)),
                  pl.BlockSpec((1, LANES), lambda i: (0, 0))],
        out_specs=pl.BlockSpec((tm, FOX_F_COLS * HEAD_DIM), lambda i: (i, 0)),
        out_shape=jax.ShapeDtypeStruct((t, FOX_F_COLS * HEAD_DIM), F32),
        scratch_shapes=[pltpu.VMEM((1, LANES), F32)],
        compiler_params=_cparams(("arbitrary",)),
        name="fox_forget_cumsum",
    )(x_bf, w_pad_bf, b_pad)


def _bcast_row(a, period, row):
    n, w = a.shape
    a3 = a.reshape(n // period, period, w)
    return jnp.broadcast_to(a3[:, row:row + 1, :], a3.shape).reshape(n, w)


def _hgrn_kernel(q_ref, k_ref, v_ref, g_ref, hg_ref, gain_ref, tri_ref, u_ref, o_ref, st_ref):
    ts = q_ref.shape[0]
    n_chunks = ts // HG_CHUNK

    @pl.when(pl.program_id(2) == 0)
    def _():
        st_ref[...] = jnp.zeros_like(st_ref)

    q = q_ref[...].astype(F32)
    k = k_ref[...].astype(F32)
    v = v_ref[...]
    g = g_ref[...] * LOG2E
    g_hi = g.astype(BF16)
    g_lo = (g - g_hi.astype(F32)).astype(BF16)
    tri = tri_ref[...]
    b = _dot(tri, g_hi) + _dot(tri, g_lo)

    row128 = lax.broadcasted_iota(jnp.int32, (ts, HEAD_DIM), 0)
    row64 = lax.broadcasted_iota(jnp.int32, (ts, HG_CHUNK), 0) % HG_CHUNK
    col64 = lax.broadcasted_iota(jnp.int32, (ts, HG_CHUNK), 1)

    a_key = b - jnp.log2(k)
    parts = []
    for s in range(HG_SUB):
        a_s = _bcast_row(a_key, HG_SUB, s)
        parts.append((q * jnp.exp2(jnp.minimum(b - a_s, 0.0))).astype(BF16))
    scores = _dot(jnp.concatenate(parts, axis=1), u_ref[...])
    same_sub = (row64 // HG_SUB) == (col64 // HG_SUB)
    scores = jnp.where(same_sub & ((col64 % HG_SUB) <= (row64 % HG_SUB)), scores, 0.0)

    level_q, level_k, level_mask = [], [], []
    m = HG_SUB
    while m < HG_CHUNK:
        ref_b = _bcast_row(b, 2 * m, m - 1)
        second = (row128 % (2 * m)) >= m
        e = jnp.exp2(jnp.where(second, b - ref_b, ref_b - b))
        level_q.append((q * e).astype(BF16))
        level_k.append((k * e).astype(BF16))
        level_mask.append(((row64 // (2 * m)) == (col64 // (2 * m)))
                          & ((row64 % (2 * m)) >= m) & ((col64 % (2 * m)) < m))
        m *= 2

    b_last = _bcast_row(b, HG_CHUNK, HG_CHUNK - 1)
    q_in = (q * jnp.exp2(b)).astype(BF16)
    k_out = (k * jnp.exp2(b_last - b)).astype(BF16)

    chunks = [slice(c * HG_CHUNK, (c + 1) * HG_CHUNK) for c in range(n_chunks)]
    intra, state_in = [], []
    for sl in chunks:
        a = scores[sl]
        for lq, lk, lm in zip(level_q, level_k, level_mask):
            a = a + jnp.where(lm[sl], _dot_nt(lq[sl], lk[sl]), 0.0)
        intra.append(_dot(a.astype(BF16), v[sl]))
        state_in.append(_dot_tn(v[sl], k_out[sl]))

    gain = gain_ref[...]
    st = st_ref[...]
    for c, sl in enumerate(chunks):
        o = intra[c] + _dot_nt(q_in[sl], st.astype(BF16))
        st = st * jnp.exp2(b[sl.stop - 1:sl.stop, :]) + state_in[c]
        ms = jnp.mean(o * o, axis=-1, keepdims=True)
        o = o * lax.rsqrt(ms + NORM_EPS) * gain * hg_ref[sl, :].astype(F32)
        o_ref[sl, :] = o.astype(o_ref.dtype)
    st_ref[...] = st


def _hgrn(q, k, v, logf, hg_act, gain, batch, seq):
    d = q.shape[-1]
    heads = d // HEAD_DIM
    ts = min(HG_TS, seq)
    shp = (batch, seq, d)
    q, k, v, logf, hg_act = (a.reshape(shp) for a in (q, k, v, logf, hg_act))
    r = jnp.arange(ts)
    tri = ((r[:, None] // HG_CHUNK == r[None, :] // HG_CHUNK) & (r[None, :] <= r[:, None])).astype(BF16)
    u_rows = jnp.arange(HG_SUB * HEAD_DIM) // HEAD_DIM
    u = (u_rows[:, None] == (jnp.arange(HG_CHUNK)[None, :] % HG_SUB)).astype(BF16)
    tile = pl.BlockSpec((None, ts, HEAD_DIM), lambda b, h, s: (b, s, h))
    out = pl.pallas_call(
        _hgrn_kernel,
        grid=(batch, heads, seq // ts),
        in_specs=[tile, tile, tile, tile, tile,
                  pl.BlockSpec((1, HEAD_DIM), lambda b, h, s: (0, 0)),
                  pl.BlockSpec((ts, ts), lambda b, h, s: (0, 0)),
                  pl.BlockSpec((HG_SUB * HEAD_DIM, HG_CHUNK), lambda b, h, s: (0, 0))],
        out_specs=tile,
        out_shape=jax.ShapeDtypeStruct(shp, BF16),
        scratch_shapes=[pltpu.VMEM((HEAD_DIM, HEAD_DIM), F32)],
        compiler_params=_cparams(("parallel", "parallel", "arbitrary")),
        name="hgrn2",
    )(q, k, v, logf, hg_act, gain, tri, u)
    return out.reshape(batch * seq, d)


def _fox_kernel(qt_ref, k_ref, vt_ref, nc_ref, o_ref, acc_ref, sc_ref):
    tq = qt_ref.shape[1]
    tk = vt_ref.shape[2]
    n_streams = acc_ref.shape[0]
    qi = pl.program_id(2)
    cw = min(FOX_CW, tq)
    n_cg = tq // cw
    acc_ref[...] = jnp.zeros_like(acc_ref)

    chains = [(s, cg) for s in range(n_streams) for cg in range(n_cg)]

    def block_starts(g):
        return [pl.multiple_of((g * n_streams + s) * tk, tk) for s in range(n_streams)]

    def score_matmuls(g):
        starts = block_starts(g)
        for ci, (s, cg) in enumerate(chains):
            sc_ref[g % 2, ci] = _dot(k_ref[pl.ds(starts[s], tk), :], qt_ref[:, cg * cw:(cg + 1) * cw])

    def group(g, carry, masked):
        starts = block_starts(g)
        ml_new, probs = [], []
        for ci, ((s, cg), (m_old, l_old)) in enumerate(zip(chains, carry)):
            if masked and s * tk >= (cg + 1) * cw:
                ml_new.append((m_old, l_old))
                probs.append(None)
                continue
            st = sc_ref[g % 2, ci]
            st = st + jnp.concatenate([nc_ref[pl.ds(starts[s], tk), :]] * (cw // HEAD_DIM), axis=1)
            if masked:
                key_pos = starts[s] + lax.broadcasted_iota(jnp.int32, st.shape, 0)
                query_pos = qi * tq + cg * cw + lax.broadcasted_iota(jnp.int32, st.shape, 1)
                st = jnp.where(key_pos <= query_pos, st, FOX_MASKED)
            m_new = jnp.maximum(m_old, jnp.max(st, axis=0, keepdims=True))
            alpha = jnp.exp2(m_old - m_new)
            p = jnp.exp2(st - m_new)
            ml_new.append((m_new, alpha * l_old + jnp.sum(p, axis=0, keepdims=True)))
            probs.append((alpha, p.astype(BF16)))
        if not masked:
            score_matmuls(g + 1)
        for (s, cg), ap in zip(chains, probs):
            if ap is not None:
                cols = slice(cg * cw, (cg + 1) * cw)
                acc_ref[s, :, cols] = acc_ref[s, :, cols] * ap[0] + _dot(vt_ref[g * n_streams + s], ap[1])
        return tuple(ml_new)

    init = ((jnp.full((1, cw), FOX_MASKED, F32), jnp.zeros((1, cw), F32)),) * len(chains)

    score_matmuls(0)
    carry = lax.fori_loop(0, qi, lambda g, c: group(g, c, False), init)
    carry = group(qi, carry, True)
    carry = tuple(tuple(carry[s * n_cg + cg] for cg in range(n_cg)) for s in range(n_streams))

    for cg in range(n_cg):
        cols = slice(cg * cw, (cg + 1) * cw)
        m = carry[0][cg][0]
        for s in range(1, n_streams):
            m = jnp.maximum(m, carry[s][cg][0])
        l = jnp.zeros((1, cw), F32)
        acc = jnp.zeros((HEAD_DIM, cw), F32)
        for s in range(n_streams):
            w = jnp.exp2(carry[s][cg][0] - m)
            l = l + w * carry[s][cg][1]
            acc = acc + w * acc_ref[s, :, cols]
        o_ref[cols, :] = (acc * (1.0 / l)).T.astype(o_ref.dtype)


def _fox(q, k, v, nc_rep, batch, seq):
    d = q.shape[-1]
    heads = d // HEAD_DIM
    tq = min(FOX_TQ, seq)
    tk = min(FOX_TK, seq)
    shp = (batch, seq, d)
    qt = q.reshape(batch, seq, heads, HEAD_DIM).transpose(0, 2, 3, 1)
    vt = v.reshape(batch, seq // tk, tk, heads, HEAD_DIM).transpose(0, 3, 1, 4, 2)
    kvfull = pl.BlockSpec((None, seq, HEAD_DIM), lambda b, h, i: (b, 0, h))
    out = pl.pallas_call(
        _fox_kernel,
        grid=(batch, heads, seq // tq),
        in_specs=[pl.BlockSpec((None, None, HEAD_DIM, tq), lambda b, h, i: (b, h, 0, i)),
                  kvfull,
                  pl.BlockSpec((None, None, seq // tk, HEAD_DIM, tk), lambda b, h, i: (b, h, 0, 0, 0)),
                  kvfull],
        out_specs=pl.BlockSpec((None, tq, HEAD_DIM), lambda b, h, i: (b, i, h)),
        out_shape=jax.ShapeDtypeStruct(shp, BF16),
        scratch_shapes=[pltpu.VMEM((tq // tk, HEAD_DIM, tq), F32),
                        pltpu.VMEM((2, (tq // tk) * (tq // min(FOX_CW, tq)), tk, min(FOX_CW, tq)), F32)],
        compiler_params=_cparams(("parallel", "parallel", "arbitrary")),
        name="fox_attention",
    )(qt, k.reshape(shp), vt, nc_rep.reshape(shp))
    return out.reshape(batch * seq, d)


def _layer_norm(z, g, b):
    mu = jnp.mean(z, axis=-1, keepdims=True)
    zc = z - mu
    var = jnp.mean(zc * zc, axis=-1, keepdims=True)
    return zc * lax.rsqrt(var + LN_EPS) * g + b


def _first_index_of_max(vals, lane):
    vmax = jnp.max(vals, axis=-1, keepdims=True)
    idx = jnp.min(jnp.where(vals == vmax, lane, LANES), axis=-1, keepdims=True)
    return vmax, idx


def _merge_kernel(alpha, oa_ref, ob_ref, ga_ref, gb_ref, x_ref, wa_ref, wb_ref, wo_ref, g_ref, b_ref,
                  wr_hi_ref, wr_lo_ref, br_ref, x1_ref, gate_ref, idx_ref):
    y_a = _dot(oa_ref[...], wa_ref[...])
    y_b = _dot(ob_ref[...], wb_ref[...])
    merged = ga_ref[...].astype(F32) * y_a + gb_ref[...].astype(F32) * y_b
    h = _dot(merged.astype(BF16), wo_ref[...])
    x1 = _layer_norm(alpha * x_ref[...] + h, g_ref[...], b_ref[...])
    x1_ref[...] = x1

    x_hi = x1.astype(BF16)
    x_lo = (x1 - x_hi.astype(F32)).astype(BF16)
    wr_hi = wr_hi_ref[...]
    logits = _dot(x_hi, wr_hi) + _dot(x_lo, wr_hi) + _dot(x_hi, wr_lo_ref[...]) + br_ref[...]
    lane = lax.broadcasted_iota(jnp.int32, logits.shape, 1)
    neg = -jnp.inf
    g_logits = jnp.where(lane < N_GROUPS, logits, neg)
    g_max, grp = _first_index_of_max(g_logits, lane)
    p_grp = 1.0 / jnp.sum(jnp.exp(g_logits - g_max), axis=-1, keepdims=True)
    lo = N_GROUPS + EXPERTS_PER_GROUP * grp
    e_logits = jnp.where((lane >= lo) & (lane < lo + EXPERTS_PER_GROUP), logits, neg)
    v1, i1 = _first_index_of_max(e_logits, lane)
    v2, i2 = _first_index_of_max(jnp.where(lane == i1, neg, e_logits), lane)
    e2 = jnp.exp(v2 - v1)
    w1 = 1.0 / (1.0 + e2)
    gate_ref[...] = jnp.where(lane == 0, w1 * p_grp, jnp.where(lane == 1, e2 * w1 * p_grp, 0.0))
    idx_ref[...] = jnp.where(lane == 0, i1 - N_GROUPS, jnp.where(lane == 1, i2 - N_GROUPS, 0))


def _merge(alpha, o_a, o_b, ga, gb, x, wa, wb, wo, ln_g, ln_b, wr_hi, wr_lo, br):
    t, d = x.shape
    tm = min(MERGE_TM, t)
    row = lambda n: pl.BlockSpec((tm, n), lambda i: (i, 0))
    full = lambda a: pl.BlockSpec(a.shape, lambda i: (0, 0))
    return pl.pallas_call(
        functools.partial(_merge_kernel, alpha),
        grid=(t // tm,),
        in_specs=[row(d), row(d), row(d), row(d), row(d), full(wa), full(wb), full(wo), full(ln_g), full(ln_b),
                  full(wr_hi), full(wr_lo), full(br)],
        out_specs=[row(d), row(LANES), row(LANES)],
        out_shape=[jax.ShapeDtypeStruct((t, d), F32), jax.ShapeDtypeStruct((t, LANES), F32),
                   jax.ShapeDtypeStruct((t, LANES), jnp.int32)],
        compiler_params=_cparams(("parallel",)),
        name="merge_ln_router",
    )(o_a, o_b, ga, gb, x, wa, wb, wo, ln_g, ln_b, wr_hi, wr_lo, br)


def _moe_kernel(blk_e_ref, src_ref, src_next_ref, dst_ref, dst_prev_ref, x_hbm, wg_ref, wu_ref, wd_ref, y_hbm,
                xbuf, ybuf, wg_bf, wu_bf, wd_bf, gsem, ssem):
    i = pl.program_id(0)
    nb = pl.num_programs(0)
    blk = xbuf.shape[1]
    slot = i % 2
    other = 1 - slot

    def gather_row(idx_ref, s, r):
        return pltpu.make_async_copy(x_hbm.at[pl.ds(idx_ref[0, r], 1), :], xbuf.at[s, pl.ds(r, 1), :], gsem.at[s])

    def scatter_row(idx_ref, s, r):
        return pltpu.make_async_copy(ybuf.at[s, pl.ds(r, 1), :], y_hbm.at[pl.ds(idx_ref[0, r], 1), :], ssem.at[s])

    def for_rows(fn):
        def body(r, carry):
            fn(r)
            return carry
        lax.fori_loop(0, blk, body, 0, unroll=8)

    def wait_gather(s):
        pltpu.make_async_copy(x_hbm.at[pl.ds(0, blk), :], xbuf.at[s], gsem.at[s]).wait()

    def wait_scatter(s):
        pltpu.make_async_copy(ybuf.at[s], y_hbm.at[pl.ds(0, blk), :], ssem.at[s]).wait()

    @pl.when(i == 0)
    def _():
        for_rows(lambda r: gather_row(src_ref, 0, r).start())
        ybuf[1] = jnp.zeros((blk, ybuf.shape[2]), F32)

    @pl.when((i == 0) | (blk_e_ref[i] != blk_e_ref[jnp.maximum(i - 1, 0)]))
    def _():
        wg_bf[...] = wg_ref[...].astype(BF16)
        wu_bf[...] = wu_ref[...].astype(BF16)
        wd_bf[...] = wd_ref[...].astype(BF16)

    wait_gather(slot)

    @pl.when(i >= 1)
    def _():
        wait_scatter(slot)

    xb = xbuf[slot].astype(BF16)
    for r in range(blk):
        gather_row(src_next_ref, other, r).start()
        scatter_row(dst_prev_ref, other, r).start()
    gate = _dot(xb, wg_bf[...])
    up = _dot(xb, wu_bf[...])
    hidden = (gate * jax.nn.sigmoid(gate) * up).astype(BF16)
    ybuf[slot] = _dot(hidden, wd_bf[...])

    @pl.when(i == nb - 1)
    def _():
        for_rows(lambda r: scatter_row(dst_ref, slot, r).start())
        wait_scatter(other)
        wait_scatter(slot)
        wait_gather(other)


def _moe(x1, blk_e, src, dst, wg, wu, wd, layer):
    t, d = x1.shape
    n_blocks = blk_e.shape[0]
    blk = src.shape[-1]
    de = wg.shape[-1]
    smem_idx = lambda fn: pl.BlockSpec((None, 1, blk), fn, memory_space=pltpu.SMEM)
    cur = lambda i, be: (i, 0, 0)
    nxt = lambda i, be: (jnp.minimum(i + 1, n_blocks - 1), 0, 0)
    prv = lambda i, be: ((i + n_blocks - 1) % n_blocks, 0, 0)
    expert = lambda i, be: (layer, be[i], 0, 0)
    grid_spec = pltpu.PrefetchScalarGridSpec(
        num_scalar_prefetch=1,
        grid=(n_blocks,),
        in_specs=[smem_idx(cur), smem_idx(nxt), smem_idx(cur), smem_idx(prv),
                  pl.BlockSpec(memory_space=pl.ANY),
                  pl.BlockSpec((None, None, d, de), expert),
                  pl.BlockSpec((None, None, d, de), expert),
                  pl.BlockSpec((None, None, de, d), expert)],
        out_specs=pl.BlockSpec(memory_space=pl.ANY),
        scratch_shapes=[pltpu.VMEM((2, blk, d), F32), pltpu.VMEM((2, blk, d), F32),
                        pltpu.VMEM((d, de), BF16), pltpu.VMEM((d, de), BF16), pltpu.VMEM((de, d), BF16),
                        pltpu.SemaphoreType.DMA((2,)), pltpu.SemaphoreType.DMA((2,))],
    )
    return pl.pallas_call(
        _moe_kernel,
        grid_spec=grid_spec,
        out_shape=jax.ShapeDtypeStruct((n_blocks * blk, d), F32),
        compiler_params=_cparams(("arbitrary",)),
        name="moe_experts",
    )(blk_e, src, src, dst, dst, x1, wg, wu, wd)


def _moe_plan(expert_idx, t):
    a = t * TOP_K
    blk = min(MOE_BLK, a)
    n_blocks = a // blk + N_EXPERTS
    n_slots = n_blocks * blk
    flat_e = expert_idx.reshape(a)
    order = jnp.argsort(flat_e).astype(jnp.int32)
    counts = jnp.bincount(flat_e, length=N_EXPERTS).astype(jnp.int32)
    padded = (counts + blk - 1) // blk * blk
    pad_end = jnp.cumsum(padded)
    pad_start = pad_end - padded
    start = jnp.cumsum(counts) - counts
    n_used = (pad_end[-1] // blk).astype(jnp.int32)
    block_start = jnp.arange(n_blocks, dtype=jnp.int32) * blk
    blk_e = jnp.sum(block_start[:, None] >= pad_end[None, :], axis=1).astype(jnp.int32)
    blk_e = jnp.minimum(blk_e, N_EXPERTS - 1)
    blk_e = jnp.where(jnp.arange(n_blocks) < n_used, blk_e, blk_e[jnp.maximum(n_used - 1, 0)])
    slot = jnp.arange(n_slots, dtype=jnp.int32)
    slot_e = jnp.repeat(blk_e, blk)
    r = slot - pad_start[slot_e]
    valid = (r < counts[slot_e]) & (slot < pad_end[-1])
    assign = order[jnp.clip(start[slot_e] + r, 0, a - 1)]
    src = jnp.where(valid, assign // TOP_K, 0)
    pad_rank = jnp.cumsum(jnp.where(valid, 0, 1)) - 1
    dst = jnp.where(valid, (assign % TOP_K) * t + assign // TOP_K,
                    jnp.where(slot >= pad_end[-1], slot, a + pad_rank))
    return (blk_e, src.reshape(n_blocks, 1, blk).astype(jnp.int32),
            dst.reshape(n_blocks, 1, blk).astype(jnp.int32))


def _combine_kernel(alpha, x_ref, y0_ref, y1_ref, gate_ref, g_ref, b_ref, o_ref, obf_ref):
    gate = gate_ref[...]
    f = gate[:, 0:1] * y0_ref[...] + gate[:, 1:2] * y1_ref[...]
    out = _layer_norm(alpha * x_ref[...] + f, g_ref[...], b_ref[...])
    o_ref[...] = out
    obf_ref[...] = out.astype(BF16)


def _combine(alpha, x1, y, gate, ln_g, ln_b):
    t, d = x1.shape
    tm = min(COMB_TM, t)
    nt = t // tm
    row = lambda n: pl.BlockSpec((tm, n), lambda i: (i, 0))
    full = lambda a: pl.BlockSpec(a.shape, lambda i: (0, 0))
    return pl.pallas_call(
        functools.partial(_combine_kernel, alpha),
        grid=(nt,),
        in_specs=[row(d), row(d), pl.BlockSpec((tm, d), lambda i: (nt + i, 0)), row(LANES), full(ln_g), full(ln_b)],
        out_specs=[row(d), row(d)],
        out_shape=[jax.ShapeDtypeStruct((t, d), F32), jax.ShapeDtypeStruct((t, d), BF16)],
        compiler_params=_cparams(("parallel",)),
        name="combine_ln",
    )(x1, y, y, gate, ln_g, ln_b)


def kernel(x, w_in, b_fox_f, fox_q_gain, fox_k_gain, hg_lb_logits, hg_norm_gain, w_br_a, w_br_b, w_out, ln1_g, ln1_b, router_g_w, router_g_b, router_e_w, router_e_b, w_gate, w_up, w_down, ln2_g, ln2_b):
    batch, seq, d = x.shape
    depth = w_in.shape[0]
    t = batch * seq
    heads = d // HEAD_DIM
    alpha = (2 * depth) ** 0.25
    assert w_in.shape[2] == 9 * d + FOX_F_COLS

    lb_p = jax.nn.softmax(hg_lb_logits.astype(F32), axis=0)
    lb_all = jnp.cumsum(lb_p, axis=0) - lb_p[0]
    fox_scale = HEAD_DIM ** -0.5 * LOG2E

    xf = x.reshape(t, d)
    xb = xf.astype(BF16)
    for l in range(depth):
        w = w_in[l].astype(BF16)
        col = lambda j: w[:, j * d:(j + 1) * d]
        w_gates = w[:, 7 * d + FOX_F_COLS:]
        w_ff = jnp.pad(w[:, 7 * d:7 * d + FOX_F_COLS], ((0, 0), (0, LANES - FOX_F_COLS)))
        b_ff = jnp.pad(b_fox_f[l], (0, LANES - FOX_F_COLS)).reshape(1, LANES)

        (q_a,) = _proj(xb, col(0), "silu")
        logf, k_a = _proj(xb, col(1), "forget", aux=(lb_all[l].reshape(1, d),), out_dtypes=(F32, BF16))
        (v_a,) = _proj(xb, col(2), "plain")
        (hg_act,) = _proj(xb, col(3), "silu")
        (q_b,) = _proj(xb, col(4), "rms", aux=(jnp.tile(fox_q_gain[l] * fox_scale, heads).reshape(1, d),))
        (k_b,) = _proj(xb, col(5), "rms", aux=(jnp.tile(fox_k_gain[l], heads).reshape(1, d),))
        (v_b,) = _proj(xb, col(6), "plain")
        (g_a,) = _proj(xb, w_gates[:, :d], "sigmoid")
        (g_b,) = _proj(xb, w_gates[:, d:], "sigmoid")
        nc_rep = _fox_forget(xb, w_ff, b_ff, seq)

        o_a = _hgrn(q_a, k_a, v_a, logf, hg_act, hg_norm_gain[l].reshape(1, HEAD_DIM), batch, seq)

        o_b = _fox(q_b, k_b, v_b, nc_rep, batch, seq)

        w_r = jnp.pad(jnp.concatenate([router_g_w[l], router_e_w[l]], axis=1),
                      ((0, 0), (0, LANES - N_GROUPS - N_EXPERTS)))
        b_r = jnp.pad(jnp.concatenate([router_g_b[l], router_e_b[l]]), (0, LANES - N_GROUPS - N_EXPERTS)).reshape(1, LANES)
        w_r_hi = w_r.astype(BF16)
        w_r_lo = (w_r - w_r_hi.astype(F32)).astype(BF16)
        x1, gate, idx = _merge(alpha, o_a, o_b, g_a, g_b, xf, w_br_a[l].astype(BF16), w_br_b[l].astype(BF16),
                               w_out[l].astype(BF16), ln1_g[l].reshape(1, d), ln1_b[l].reshape(1, d),
                               w_r_hi, w_r_lo, b_r)

        blk_e, src, dst = _moe_plan(idx[:, :TOP_K], t)
        y = _moe(x1, blk_e, src, dst, w_gate, w_up, w_down, l)
        xf, xb = _combine(alpha, x1, y, gate, ln2_g[l].reshape(1, d), ln2_b[l].reshape(1, d))
    return xf.reshape(batch, seq, d)
```

```python
import functools

import jax
import jax.numpy as jnp
from jax import lax
from jax.experimental import pallas as pl
from jax.experimental.pallas import tpu as pltpu

F32 = jnp.float32
BF16 = jnp.bfloat16

HEAD_DIM = 128
FOX_F_COLS = 8
N_GROUPS = 4
EXPERTS_PER_GROUP = 8
N_EXPERTS = N_GROUPS * EXPERTS_PER_GROUP
TOP_K = 2
LN_EPS = 1e-5
NORM_EPS = 1e-6
LOG2E = 1.4426950408889634
FOX_MASKED = -1e30

LANES = 128
SUBLANES = 8
VMEM_LIMIT_BYTES = 48 * 1024 * 1024

PROJ_TM = 1024
HG_CHUNK = 64
HG_SUB = 8
HG_TS = 256
FOX_TQ = 512
FOX_TK = 256
FOX_CW = 256
FOX_VPAD = 16
MERGE_TM = 512
MOE_BLK = 256
COMB_TM = 512


def _cparams(sem):
    return pltpu.CompilerParams(dimension_semantics=sem, vmem_limit_bytes=VMEM_LIMIT_BYTES)


def _split3(a):
    hi = a.astype(BF16)
    r1 = a - hi.astype(F32)
    mid = r1.astype(BF16)
    lo = (r1 - mid.astype(F32)).astype(BF16)
    return hi, mid, lo


def _dot(a, b):
    return jnp.dot(a, b, preferred_element_type=F32)


def _dot_nt(a, b):
    return lax.dot_general(a, b, (((1,), (1,)), ((), ())), preferred_element_type=F32)


def _dot_tn(a, b):
    return lax.dot_general(a, b, (((0,), (0,)), ((), ())), preferred_element_type=F32)


def _proj_kernel(kind, x_ref, w_ref, *refs):
    acc = _dot(x_ref[...], w_ref[...])
    if kind == "plain":
        (o_ref,) = refs
        o_ref[...] = acc.astype(o_ref.dtype)
    elif kind == "silu":
        (o_ref,) = refs
        o_ref[...] = (acc * jax.nn.sigmoid(acc)).astype(o_ref.dtype)
    elif kind == "sigmoid":
        (o_ref,) = refs
        o_ref[...] = jax.nn.sigmoid(acc).astype(o_ref.dtype)
    elif kind == "forget":
        lb_ref, logf_ref, k_ref = refs
        lb = lb_ref[...]
        f = lb + (1.0 - lb) * jax.nn.sigmoid(acc)
        logf_ref[...] = jnp.log(f)
        k_ref[...] = (1.0 - f).astype(k_ref.dtype)
    elif kind == "rms":
        gain_ref, o_ref = refs
        for h in range(acc.shape[1] // HEAD_DIM):
            sl = slice(h * HEAD_DIM, (h + 1) * HEAD_DIM)
            a = acc[:, sl]
            ms = jnp.mean(a * a, axis=-1, keepdims=True)
            o_ref[:, sl] = (a * lax.rsqrt(ms + NORM_EPS) * gain_ref[:, sl]).astype(o_ref.dtype)
    else:
        raise ValueError(kind)


def _proj(x_bf, w_bf, kind, aux=(), out_dtypes=(BF16,)):
    t, d = x_bf.shape
    n = w_bf.shape[1]
    tm = min(PROJ_TM, t)
    in_specs = [pl.BlockSpec((tm, d), lambda i: (i, 0)), pl.BlockSpec((d, n), lambda i: (0, 0))]
    in_specs += [pl.BlockSpec((1, n), lambda i: (0, 0)) for _ in aux]
    out_specs = [pl.BlockSpec((tm, n), lambda i: (i, 0)) for _ in out_dtypes]
    out_shape = [jax.ShapeDtypeStruct((t, n), dt) for dt in out_dtypes]
    outs = pl.pallas_call(
        functools.partial(_proj_kernel, kind),
        grid=(t // tm,),
        in_specs=in_specs,
        out_specs=out_specs,
        out_shape=out_shape,
        compiler_params=_cparams(("parallel",)),
        name=f"proj_{kind}",
    )(x_bf, w_bf, *aux)
    return outs


def _fox_forget_kernel(tiles_per_seq, x_ref, w_ref, b_ref, nc_ref, carry_ref):
    i = pl.program_id(0)

    @pl.when(i % tiles_per_seq == 0)
    def _():
        carry_ref[...] = jnp.zeros_like(carry_ref)

    z = _dot(x_ref[...], w_ref[...]) + b_ref[...]
    lf = jnp.minimum(z, 0.0) - jnp.log(1.0 + jnp.exp(-jnp.abs(z)))
    tm = lf.shape[0]
    row = lax.broadcasted_iota(jnp.int32, (tm, tm), 0)
    col = lax.broadcasted_iota(jnp.int32, (tm, tm), 1)
    tri = jnp.where(col <= row, 1.0, 0.0).astype(BF16)
    hi, mid, lo = _split3(lf)
    c = _dot(tri, hi) + _dot(tri, mid) + _dot(tri, lo) + carry_ref[...]
    carry_ref[...] = c[tm - 1:tm, :]
    for h in range(nc_ref.shape[1] // HEAD_DIM):
        nc_ref[:, h * HEAD_DIM:(h + 1) * HEAD_DIM] = jnp.broadcast_to(c[:, h:h + 1] * (-LOG2E), (tm, HEAD_DIM))


def _fox_forget(x_bf, w_pad_bf, b_pad, seq):
    t, d = x_bf.shape
    tm = min(512, seq)
    return pl.pallas_call(
        functools.partial(_fox_forget_kernel, seq // tm),
        grid=(t // tm,),
        in_specs=[pl.BlockSpec((tm, d), lambda i: (i, 0)),
                  pl.BlockSpec((d, LANES), lambda i: (0, 0)),
                  pl.BlockSpec((1, LANES), lambda i: (0, 0))],
        out_specs=pl.BlockSpec((tm, FOX_F_COLS * HEAD_DIM), lambda i: (i, 0)),
        out_shape=jax.ShapeDtypeStruct((t, FOX_F_COLS * HEAD_DIM), F32),
        scratch_shapes=[pltpu.VMEM((1, LANES), F32)],
        compiler_params=_cparams(("arbitrary",)),
        name="fox_forget_cumsum",
    )(x_bf, w_pad_bf, b_pad)


def _bcast_row(a, period, row):
    n, w = a.shape
    a3 = a.reshape(n // period, period, w)
    return jnp.broadcast_to(a3[:, row:row + 1, :], a3.shape).reshape(n, w)


def _hgrn_kernel(q_ref, k_ref, v_ref, g_ref, hg_ref, gain_ref, tri_ref, u_ref, o_ref, st_ref):
    ts = q_ref.shape[0]
    n_chunks = ts // HG_CHUNK

    @pl.when(pl.program_id(2) == 0)
    def _():
        st_ref[...] = jnp.zeros_like(st_ref)

    q = q_ref[...].astype(F32)
    k = k_ref[...].astype(F32)
    v = v_ref[...]
    g = g_ref[...] * LOG2E
    g_hi = g.astype(BF16)
    g_lo = (g - g_hi.astype(F32)).astype(BF16)
    tri = tri_ref[...]
    b = _dot(tri, g_hi) + _dot(tri, g_lo)

    row128 = lax.broadcasted_iota(jnp.int32, (ts, HEAD_DIM), 0)
    row64 = lax.broadcasted_iota(jnp.int32, (ts, HG_CHUNK), 0) % HG_CHUNK
    col64 = lax.broadcasted_iota(jnp.int32, (ts, HG_CHUNK), 1)

    a_key = b - jnp.log2(k)
    parts = []
    for s in range(HG_SUB):
        a_s = _bcast_row(a_key, HG_SUB, s)
        parts.append((q * jnp.exp2(jnp.minimum(b - a_s, 0.0))).astype(BF16))
    scores = _dot(jnp.concatenate(parts, axis=1), u_ref[...])
    same_sub = (row64 // HG_SUB) == (col64 // HG_SUB)
    scores = jnp.where(same_sub & ((col64 % HG_SUB) <= (row64 % HG_SUB)), scores, 0.0)

    level_q, level_k, level_mask = [], [], []
    m = HG_SUB
    while m < HG_CHUNK:
        ref_b = _bcast_row(b, 2 * m, m - 1)
        second = (row128 % (2 * m)) >= m
        e = jnp.exp2(jnp.where(second, b - ref_b, ref_b - b))
        level_q.append((q * e).astype(BF16))
        level_k.append((k * e).astype(BF16))
        level_mask.append(((row64 // (2 * m)) == (col64 // (2 * m)))
                          & ((row64 % (2 * m)) >= m) & ((col64 % (2 * m)) < m))
        m *= 2

    b_last = _bcast_row(b, HG_CHUNK, HG_CHUNK - 1)
    q_in = (q * jnp.exp2(b)).astype(BF16)
    k_out = (k * jnp.exp2(b_last - b)).astype(BF16)

    chunks = [slice(c * HG_CHUNK, (c + 1) * HG_CHUNK) for c in range(n_chunks)]
    intra, state_in = [], []
    for sl in chunks:
        a = scores[sl]
        for lq, lk, lm in zip(level_q, level_k, level_mask):
            a = a + jnp.where(lm[sl], _dot_nt(lq[sl], lk[sl]), 0.0)
        intra.append(_dot(a.astype(BF16), v[sl]))
        state_in.append(_dot_tn(v[sl], k_out[sl]))

    gain = gain_ref[...]
    st = st_ref[...]
    for c, sl in enumerate(chunks):
        o = intra[c] + _dot_nt(q_in[sl], st.astype(BF16))
        st = st * jnp.exp2(b[sl.stop - 1:sl.stop, :]) + state_in[c]
        ms = jnp.mean(o * o, axis=-1, keepdims=True)
        o = o * lax.rsqrt(ms + NORM_EPS) * gain * hg_ref[sl, :].astype(F32)
        o_ref[sl, :] = o.astype(o_ref.dtype)
    st_ref[...] = st


def _hgrn(q, k, v, logf, hg_act, gain, batch, seq):
    d = q.shape[-1]
    heads = d // HEAD_DIM
    ts = min(HG_TS, seq)
    shp = (batch, seq, d)
    q, k, v, logf, hg_act = (a.reshape(shp) for a in (q, k, v, logf, hg_act))
    r = jnp.arange(ts)
    tri = ((r[:, None] // HG_CHUNK == r[None, :] // HG_CHUNK) & (r[None, :] <= r[:, None])).astype(BF16)
    u_rows = jnp.arange(HG_SUB * HEAD_DIM) // HEAD_DIM
    u = (u_rows[:, None] == (jnp.arange(HG_CHUNK)[None, :] % HG_SUB)).astype(BF16)
    tile = pl.BlockSpec((None, ts, HEAD_DIM), lambda b, h, s: (b, s, h))
    out = pl.pallas_call(
        _hgrn_kernel,
        grid=(batch, heads, seq // ts),
        in_specs=[tile, tile, tile, tile, tile,
                  pl.BlockSpec((1, HEAD_DIM), lambda b, h, s: (0, 0)),
                  pl.BlockSpec((ts, ts), lambda b, h, s: (0, 0)),
                  pl.BlockSpec((HG_SUB * HEAD_DIM, HG_CHUNK), lambda b, h, s: (0, 0))],
        out_specs=tile,
        out_shape=jax.ShapeDtypeStruct(shp, BF16),
        scratch_shapes=[pltpu.VMEM((HEAD_DIM, HEAD_DIM), F32)],
        compiler_params=_cparams(("parallel", "parallel", "arbitrary")),
        name="hgrn2",
    )(q, k, v, logf, hg_act, gain, tri, u)
    return out.reshape(batch * seq, d)


def _fox_kernel(qt_ref, k_ref, vt_ref, nc_ref, o_ref, acc_ref, sc_ref):
    tq = qt_ref.shape[1]
    tk = vt_ref.shape[2]
    n_streams = acc_ref.shape[0]
    qi = pl.program_id(2)
    cw = min(FOX_CW, tq)
    n_cg = tq // cw
    acc_ref[...] = jnp.zeros_like(acc_ref)

    chains = [(s, cg) for s in range(n_streams) for cg in range(n_cg)]

    def block_starts(g):
        return [pl.multiple_of((g * n_streams + s) * tk, tk) for s in range(n_streams)]

    def score_matmuls(g):
        starts = block_starts(g)
        for ci, (s, cg) in enumerate(chains):
            sc_ref[g % 2, ci] = _dot(k_ref[pl.ds(starts[s], tk), :], qt_ref[:, cg * cw:(cg + 1) * cw])

    def group(g, carry, masked):
        starts = block_starts(g)
        m_all, probs = [], []
        for ci, ((s, cg), m_old) in enumerate(zip(chains, carry)):
            if masked and s * tk >= (cg + 1) * cw:
                m_all.append(m_old)
                probs.append(None)
                continue
            st = sc_ref[g % 2, ci]
            st = st + jnp.concatenate([nc_ref[pl.ds(starts[s], tk), :]] * (cw // HEAD_DIM), axis=1)
            if masked:
                key_pos = starts[s] + lax.broadcasted_iota(jnp.int32, st.shape, 0)
                query_pos = qi * tq + cg * cw + lax.broadcasted_iota(jnp.int32, st.shape, 1)
                st = jnp.where(key_pos <= query_pos, st, FOX_MASKED)
            m_new = jnp.maximum(m_old, jnp.max(st, axis=0, keepdims=True))
            m_all.append(m_new)
            probs.append((jnp.exp2(m_old - m_new), jnp.exp2((st - m_new).astype(BF16))))
        if not masked:
            score_matmuls(g + 1)
        for (s, cg), ap in zip(chains, probs):
            if ap is not None:
                cols = slice(cg * cw, (cg + 1) * cw)
                acc_ref[s, :, cols] = acc_ref[s, :, cols] * ap[0] + _dot(vt_ref[g * n_streams + s], ap[1])
        return tuple(m_all)

    init = (jnp.full((1, cw), FOX_MASKED, F32),) * len(chains)

    score_matmuls(0)
    carry = lax.fori_loop(0, qi, lambda g, c: group(g, c, False), init)
    carry = group(qi, carry, True)
    carry = tuple(tuple(carry[s * n_cg + cg] for cg in range(n_cg)) for s in range(n_streams))

    for cg in range(n_cg):
        cols = slice(cg * cw, (cg + 1) * cw)
        m = carry[0][cg]
        for s in range(1, n_streams):
            m = jnp.maximum(m, carry[s][cg])
        acc = jnp.zeros((acc_ref.shape[1], cw), F32)
        for s in range(n_streams):
            acc = acc + jnp.exp2(carry[s][cg] - m) * acc_ref[s, :, cols]
        out = acc[:HEAD_DIM] * (1.0 / acc[HEAD_DIM:HEAD_DIM + 1])
        o_ref[cols, :] = out.T.astype(o_ref.dtype)


def _fox(q, k, v, nc_rep, batch, seq):
    d = q.shape[-1]
    heads = d // HEAD_DIM
    tq = min(FOX_TQ, seq)
    tk = min(FOX_TK, seq)
    shp = (batch, seq, d)
    qt = q.reshape(batch, seq, heads, HEAD_DIM).transpose(0, 2, 3, 1)
    vt = v.reshape(batch, seq // tk, tk, heads, HEAD_DIM).transpose(0, 3, 1, 4, 2)
    extra = jnp.zeros(vt.shape[:3] + (FOX_VPAD, tk), BF16).at[:, :, :, 0, :].set(1.0)
    vt = jnp.concatenate([vt, extra], axis=3)
    vrows = HEAD_DIM + FOX_VPAD
    kvfull = pl.BlockSpec((None, seq, HEAD_DIM), lambda b, h, i: (b, 0, h))
    out = pl.pallas_call(
        _fox_kernel,
        grid=(batch, heads, seq // tq),
        in_specs=[pl.BlockSpec((None, None, HEAD_DIM, tq), lambda b, h, i: (b, h, 0, i)),
                  kvfull,
                  pl.BlockSpec((None, None, seq // tk, vrows, tk), lambda b, h, i: (b, h, 0, 0, 0)),
                  kvfull],
        out_specs=pl.BlockSpec((None, tq, HEAD_DIM), lambda b, h, i: (b, i, h)),
        out_shape=jax.ShapeDtypeStruct(shp, BF16),
        scratch_shapes=[pltpu.VMEM((tq // tk, vrows, tq), F32),
                        pltpu.VMEM((2, (tq // tk) * (tq // min(FOX_CW, tq)), tk, min(FOX_CW, tq)), F32)],
        compiler_params=_cparams(("parallel", "parallel", "arbitrary")),
        name="fox_attention",
    )(qt, k.reshape(shp), vt, nc_rep.reshape(shp))
    return out.reshape(batch * seq, d)


def _store_row_tiles(ref, value):
    for j in range(ref.shape[1]):
        ref[:, j, :] = value[:, j * LANES:(j + 1) * LANES]


def _load_row_tiles(ref):
    return jnp.concatenate([ref[:, j, :] for j in range(ref.shape[1])], axis=1)


def _layer_norm(z, g, b):
    mu = jnp.mean(z, axis=-1, keepdims=True)
    zc = z - mu
    var = jnp.mean(zc * zc, axis=-1, keepdims=True)
    return zc * lax.rsqrt(var + LN_EPS) * g + b


def _first_index_of_max(vals, lane):
    vmax = jnp.max(vals, axis=-1, keepdims=True)
    idx = jnp.min(jnp.where(vals == vmax, lane, LANES), axis=-1, keepdims=True)
    return vmax, idx


def _merge_kernel(alpha, oa_ref, ob_ref, ga_ref, gb_ref, x_ref, wa_ref, wb_ref, wo_ref, g_ref, b_ref,
                  wr_hi_ref, wr_lo_ref, br_ref, x1_ref, gate_ref, idx_ref):
    y_a = _dot(oa_ref[...], wa_ref[...])
    y_b = _dot(ob_ref[...], wb_ref[...])
    merged = ga_ref[...].astype(F32) * y_a + gb_ref[...].astype(F32) * y_b
    h = _dot(merged.astype(BF16), wo_ref[...])
    x1 = _layer_norm(alpha * x_ref[...] + h, g_ref[...], b_ref[...])
    _store_row_tiles(x1_ref, x1)

    x_hi = x1.astype(BF16)
    x_lo = (x1 - x_hi.astype(F32)).astype(BF16)
    wr_hi = wr_hi_ref[...]
    logits = _dot(x_hi, wr_hi) + _dot(x_lo, wr_hi) + _dot(x_hi, wr_lo_ref[...]) + br_ref[...]
    lane = lax.broadcasted_iota(jnp.int32, logits.shape, 1)
    neg = -jnp.inf
    g_logits = jnp.where(lane < N_GROUPS, logits, neg)
    g_max, grp = _first_index_of_max(g_logits, lane)
    p_grp = 1.0 / jnp.sum(jnp.exp(g_logits - g_max), axis=-1, keepdims=True)
    lo = N_GROUPS + EXPERTS_PER_GROUP * grp
    e_logits = jnp.where((lane >= lo) & (lane < lo + EXPERTS_PER_GROUP), logits, neg)
    v1, i1 = _first_index_of_max(e_logits, lane)
    v2, i2 = _first_index_of_max(jnp.where(lane == i1, neg, e_logits), lane)
    e2 = jnp.exp(v2 - v1)
    w1 = 1.0 / (1.0 + e2)
    gate_ref[...] = jnp.where(lane == 0, w1 * p_grp, jnp.where(lane == 1, e2 * w1 * p_grp, 0.0))
    idx_ref[...] = jnp.where(lane == 0, i1 - N_GROUPS, jnp.where(lane == 1, i2 - N_GROUPS, 0))


def _merge(alpha, o_a, o_b, ga, gb, x, wa, wb, wo, ln_g, ln_b, wr_hi, wr_lo, br):
    t, d = x.shape
    tm = min(MERGE_TM, t)
    row = lambda n: pl.BlockSpec((tm, n), lambda i: (i, 0))
    full = lambda a: pl.BlockSpec(a.shape, lambda i: (0, 0))
    return pl.pallas_call(
        functools.partial(_merge_kernel, alpha),
        grid=(t // tm,),
        in_specs=[row(d), row(d), row(d), row(d), row(d), full(wa), full(wb), full(wo), full(ln_g), full(ln_b),
                  full(wr_hi), full(wr_lo), full(br)],
        out_specs=[pl.BlockSpec((tm, d // LANES, LANES), lambda i: (i, 0, 0)), row(LANES), row(LANES)],
        out_shape=[jax.ShapeDtypeStruct((t, d // LANES, LANES), F32), jax.ShapeDtypeStruct((t, LANES), F32),
                   jax.ShapeDtypeStruct((t, LANES), jnp.int32)],
        compiler_params=_cparams(("parallel",)),
        name="merge_ln_router",
    )(o_a, o_b, ga, gb, x, wa, wb, wo, ln_g, ln_b, wr_hi, wr_lo, br)


def _moe_kernel(blk_e_ref, src_ref, src_next_ref, dst_ref, dst_prev_ref, x_hbm, wg_ref, wu_ref, wd_ref, y_hbm,
                xbuf, ybuf, wg_bf, wu_bf, wd_bf, gsem, ssem):
    i = pl.program_id(0)
    nb = pl.num_programs(0)
    blk = xbuf.shape[1]
    slot = i % 2
    other = 1 - slot

    def gather_row(idx_ref, s, r):
        return pltpu.make_async_copy(x_hbm.at[idx_ref[0, r]], xbuf.at[s, r], gsem.at[s])

    def scatter_row(idx_ref, s, r):
        return pltpu.make_async_copy(ybuf.at[s, r], y_hbm.at[idx_ref[0, r]], ssem.at[s])

    def for_rows(fn):
        def body(r, carry):
            fn(r)
            return carry
        lax.fori_loop(0, blk, body, 0, unroll=8)

    def wait_gather(s):
        pltpu.make_async_copy(x_hbm.at[pl.ds(0, blk)], xbuf.at[s], gsem.at[s]).wait()

    def wait_scatter(s):
        pltpu.make_async_copy(ybuf.at[s], y_hbm.at[pl.ds(0, blk)], ssem.at[s]).wait()

    @pl.when(i == 0)
    def _():
        for_rows(lambda r: gather_row(src_ref, 0, r).start())
        ybuf[1] = jnp.zeros(ybuf.shape[1:], F32)

    @pl.when((i == 0) | (blk_e_ref[i] != blk_e_ref[jnp.maximum(i - 1, 0)]))
    def _():
        wg_bf[...] = wg_ref[...].astype(BF16)
        wu_bf[...] = wu_ref[...].astype(BF16)
        wd_bf[...] = wd_ref[...].astype(BF16)

    wait_gather(slot)

    @pl.when(i >= 1)
    def _():
        wait_scatter(slot)

    xb = _load_row_tiles(xbuf.at[slot]).astype(BF16)
    for r in range(blk):
        gather_row(src_next_ref, other, r).start()
        scatter_row(dst_prev_ref, other, r).start()
    gate = _dot(xb, wg_bf[...])
    up = _dot(xb, wu_bf[...])
    hidden = (gate * jax.nn.sigmoid(gate) * up).astype(BF16)
    _store_row_tiles(ybuf.at[slot], _dot(hidden, wd_bf[...]))

    @pl.when(i == nb - 1)
    def _():
        for_rows(lambda r: scatter_row(dst_ref, slot, r).start())
        wait_scatter(other)
        wait_scatter(slot)
        wait_gather(other)


def _moe(x1, blk_e, src, dst, wg, wu, wd, layer):
    t, nj, _ = x1.shape
    d = nj * LANES
    n_blocks = blk_e.shape[0]
    blk = src.shape[-1]
    de = wg.shape[-1]
    smem_idx = lambda fn: pl.BlockSpec((None, 1, blk), fn, memory_space=pltpu.SMEM)
    cur = lambda i, be: (i, 0, 0)
    nxt = lambda i, be: (jnp.minimum(i + 1, n_blocks - 1), 0, 0)
    prv = lambda i, be: ((i + n_blocks - 1) % n_blocks, 0, 0)
    expert = lambda i, be: (layer, be[i], 0, 0)
    grid_spec = pltpu.PrefetchScalarGridSpec(
        num_scalar_prefetch=1,
        grid=(n_blocks,),
        in_specs=[smem_idx(cur), smem_idx(nxt), smem_idx(cur), smem_idx(prv),
                  pl.BlockSpec(memory_space=pl.ANY),
                  pl.BlockSpec((None, None, d, de), expert),
                  pl.BlockSpec((None, None, d, de), expert),
                  pl.BlockSpec((None, None, de, d), expert)],
        out_specs=pl.BlockSpec(memory_space=pl.ANY),
        scratch_shapes=[pltpu.VMEM((2, blk, nj, LANES), F32), pltpu.VMEM((2, blk, nj, LANES), F32),
                        pltpu.VMEM((d, de), BF16), pltpu.VMEM((d, de), BF16), pltpu.VMEM((de, d), BF16),
                        pltpu.SemaphoreType.DMA((2,)), pltpu.SemaphoreType.DMA((2,))],
    )
    return pl.pallas_call(
        _moe_kernel,
        grid_spec=grid_spec,
        out_shape=jax.ShapeDtypeStruct((n_blocks * blk, nj, LANES), F32),
        compiler_params=_cparams(("arbitrary",)),
        name="moe_experts",
    )(blk_e, src, src, dst, dst, x1, wg, wu, wd)


def _moe_plan(expert_idx, t):
    a = t * TOP_K
    blk = min(MOE_BLK, a)
    n_blocks = a // blk + N_EXPERTS
    n_slots = n_blocks * blk
    flat_e = expert_idx.reshape(a)
    order = jnp.argsort(flat_e).astype(jnp.int32)
    counts = jnp.bincount(flat_e, length=N_EXPERTS).astype(jnp.int32)
    padded = (counts + blk - 1) // blk * blk
    pad_end = jnp.cumsum(padded)
    pad_start = pad_end - padded
    start = jnp.cumsum(counts) - counts
    n_used = (pad_end[-1] // blk).astype(jnp.int32)
    block_start = jnp.arange(n_blocks, dtype=jnp.int32) * blk
    blk_e = jnp.sum(block_start[:, None] >= pad_end[None, :], axis=1).astype(jnp.int32)
    blk_e = jnp.minimum(blk_e, N_EXPERTS - 1)
    blk_e = jnp.where(jnp.arange(n_blocks) < n_used, blk_e, blk_e[jnp.maximum(n_used - 1, 0)])
    slot = jnp.arange(n_slots, dtype=jnp.int32)
    slot_e = jnp.repeat(blk_e, blk)
    r = slot - pad_start[slot_e]
    valid = (r < counts[slot_e]) & (slot < pad_end[-1])
    assign = order[jnp.clip(start[slot_e] + r, 0, a - 1)]
    src = jnp.where(valid, assign // TOP_K, 0)
    pad_rank = jnp.cumsum(jnp.where(valid, 0, 1)) - 1
    dst = jnp.where(valid, (assign % TOP_K) * t + assign // TOP_K,
                    jnp.where(slot >= pad_end[-1], slot, a + pad_rank))
    return (blk_e, src.reshape(n_blocks, 1, blk).astype(jnp.int32),
            dst.reshape(n_blocks, 1, blk).astype(jnp.int32))


def _combine_kernel(alpha, x_ref, y0_ref, y1_ref, gate_ref, g_ref, b_ref, o_ref, obf_ref):
    gate = gate_ref[...]
    f = gate[:, 0:1] * _load_row_tiles(y0_ref) + gate[:, 1:2] * _load_row_tiles(y1_ref)
    out = _layer_norm(alpha * _load_row_tiles(x_ref) + f, g_ref[...], b_ref[...])
    o_ref[...] = out
    obf_ref[...] = out.astype(BF16)


def _combine(alpha, x1, y, gate, ln_g, ln_b):
    t, nj, _ = x1.shape
    d = nj * LANES
    tm = min(COMB_TM, t)
    nt = t // tm
    row = lambda n: pl.BlockSpec((tm, n), lambda i: (i, 0))
    tiles = lambda off: pl.BlockSpec((tm, nj, LANES), lambda i: (off + i, 0, 0))
    full = lambda a: pl.BlockSpec(a.shape, lambda i: (0, 0))
    return pl.pallas_call(
        functools.partial(_combine_kernel, alpha),
        grid=(nt,),
        in_specs=[tiles(0), tiles(0), tiles(nt), row(LANES), full(ln_g), full(ln_b)],
        out_specs=[row(d), row(d)],
        out_shape=[jax.ShapeDtypeStruct((t, d), F32), jax.ShapeDtypeStruct((t, d), BF16)],
        compiler_params=_cparams(("parallel",)),
        name="combine_ln",
    )(x1, y, y, gate, ln_g, ln_b)


def kernel(x, w_in, b_fox_f, fox_q_gain, fox_k_gain, hg_lb_logits, hg_norm_gain, w_br_a, w_br_b, w_out, ln1_g, ln1_b, router_g_w, router_g_b, router_e_w, router_e_b, w_gate, w_up, w_down, ln2_g, ln2_b):
    batch, seq, d = x.shape
    depth = w_in.shape[0]
    t = batch * seq
    heads = d // HEAD_DIM
    alpha = (2 * depth) ** 0.25
    assert w_in.shape[2] == 9 * d + FOX_F_COLS

    lb_p = jax.nn.softmax(hg_lb_logits.astype(F32), axis=0)
    lb_all = jnp.cumsum(lb_p, axis=0) - lb_p[0]
    fox_scale = HEAD_DIM ** -0.5 * LOG2E

    xf = x.reshape(t, d)
    xb = xf.astype(BF16)
    for l in range(depth):
        w = w_in[l].astype(BF16)
        col = lambda j: w[:, j * d:(j + 1) * d]
        w_gates = w[:, 7 * d + FOX_F_COLS:]
        w_ff = jnp.pad(w[:, 7 * d:7 * d + FOX_F_COLS], ((0, 0), (0, LANES - FOX_F_COLS)))
        b_ff = jnp.pad(b_fox_f[l], (0, LANES - FOX_F_COLS)).reshape(1, LANES)

        (q_a,) = _proj(xb, col(0), "silu")
        logf, k_a = _proj(xb, col(1), "forget", aux=(lb_all[l].reshape(1, d),), out_dtypes=(F32, BF16))
        (v_a,) = _proj(xb, col(2), "plain")
        (hg_act,) = _proj(xb, col(3), "silu")
        (q_b,) = _proj(xb, col(4), "rms", aux=(jnp.tile(fox_q_gain[l] * fox_scale, heads).reshape(1, d),))
        (k_b,) = _proj(xb, col(5), "rms", aux=(jnp.tile(fox_k_gain[l], heads).reshape(1, d),))
        (v_b,) = _proj(xb, col(6), "plain")
        (g_a,) = _proj(xb, w_gates[:, :d], "sigmoid")
        (g_b,) = _proj(xb, w_gates[:, d:], "sigmoid")
        nc_rep = _fox_forget(xb, w_ff, b_ff, seq)

        o_a = _hgrn(q_a, k_a, v_a, logf, hg_act, hg_norm_gain[l].reshape(1, HEAD_DIM), batch, seq)

        o_b = _fox(q_b, k_b, v_b, nc_rep, batch, seq)

        w_r = jnp.pad(jnp.concatenate([router_g_w[l], router_e_w[l]], axis=1),
                      ((0, 0), (0, LANES - N_GROUPS - N_EXPERTS)))
        b_r = jnp.pad(jnp.concatenate([router_g_b[l], router_e_b[l]]), (0, LANES - N_GROUPS - N_EXPERTS)).reshape(1, LANES)
        w_r_hi = w_r.astype(BF16)
        w_r_lo = (w_r - w_r_hi.astype(F32)).astype(BF16)
        x1, gate, idx = _merge(alpha, o_a, o_b, g_a, g_b, xf, w_br_a[l].astype(BF16), w_br_b[l].astype(BF16),
                               w_out[l].astype(BF16), ln1_g[l].reshape(1, d), ln1_b[l].reshape(1, d),
                               w_r_hi, w_r_lo, b_r)

        blk_e, src, dst = _moe_plan(idx[:, :TOP_K], t)
        y = _moe(x1, blk_e, src, dst, w_gate, w_up, w_down, l)
        xf, xb = _combine(alpha, x1, y, gate, ln2_g[l].reshape(1, d), ln2_b[l].reshape(1, d))
    return xf.reshape(batch, seq, d)
```

```python
import functools

import jax
import jax.numpy as jnp
from jax import lax
from jax.experimental import pallas as pl
from jax.experimental.pallas import tpu as pltpu

F32 = jnp.float32
BF16 = jnp.bfloat16

HEAD_DIM = 128
FOX_F_COLS = 8
N_GROUPS = 4
EXPERTS_PER_GROUP = 8
N_EXPERTS = N_GROUPS * EXPERTS_PER_GROUP
TOP_K = 2
LN_EPS = 1e-5
NORM_EPS = 1e-6
LOG2E = 1.4426950408889634
FOX_MASKED = -1e30

LANES = 128
SUBLANES = 8
VMEM_LIMIT_BYTES = 48 * 1024 * 1024

PROJ_TM = 1024
HG_CHUNK = 64
HG_SUB = 8
HG_TS = 256
FOX_TQ = 512
FOX_TK = 256
FOX_CW = 256
FOX_VPAD = 16
MERGE_TM = 512
MOE_BLK = 256
COMB_TM = 512


def _cparams(sem):
    return pltpu.CompilerParams(dimension_semantics=sem, vmem_limit_bytes=VMEM_LIMIT_BYTES)


def _split3(a):
    hi = a.astype(BF16)
    r1 = a - hi.astype(F32)
    mid = r1.astype(BF16)
    lo = (r1 - mid.astype(F32)).astype(BF16)
    return hi, mid, lo


def _dot(a, b):
    return jnp.dot(a, b, preferred_element_type=F32)


def _dot_nt(a, b):
    return lax.dot_general(a, b, (((1,), (1,)), ((), ())), preferred_element_type=F32)


def _dot_tn(a, b):
    return lax.dot_general(a, b, (((0,), (0,)), ((), ())), preferred_element_type=F32)


SEG_Q_A, SEG_K_A, SEG_V_A, SEG_HG, SEG_Q_B, SEG_K_B, SEG_V_B, SEG_G_A, SEG_G_B = range(9)
SEG_KINDS = ("silu", "forget", "plain", "silu", "rms", "rms", "plain", "sigmoid", "sigmoid")


def _proj_fused_kernel(x_ref, w_ref, aux_ref, o_ref, logf_ref):
    j = pl.program_id(1)
    acc = _dot(x_ref[...], w_ref[...])

    def is_kind(kind):
        cond = None
        for seg, k in enumerate(SEG_KINDS):
            if k == kind:
                cond = (j == seg) if cond is None else (cond | (j == seg))
        return cond

    @pl.when(is_kind("plain"))
    def _():
        o_ref[...] = acc.astype(o_ref.dtype)

    @pl.when(is_kind("silu"))
    def _():
        o_ref[...] = (acc * jax.nn.sigmoid(acc)).astype(o_ref.dtype)

    @pl.when(is_kind("sigmoid"))
    def _():
        o_ref[...] = jax.nn.sigmoid(acc).astype(o_ref.dtype)

    @pl.when(is_kind("forget"))
    def _():
        lb = aux_ref[...]
        f = lb + (1.0 - lb) * jax.nn.sigmoid(acc)
        logf_ref[...] = jnp.log(f)
        o_ref[...] = (1.0 - f).astype(o_ref.dtype)

    @pl.when(is_kind("rms"))
    def _():
        for h in range(acc.shape[1] // HEAD_DIM):
            sl = slice(h * HEAD_DIM, (h + 1) * HEAD_DIM)
            a = acc[:, sl]
            ms = jnp.mean(a * a, axis=-1, keepdims=True)
            o_ref[:, sl] = (a * lax.rsqrt(ms + NORM_EPS) * aux_ref[:, sl]).astype(o_ref.dtype)


def _proj_fused(x_bf, w_cat, aux):
    t, d = x_bf.shape
    n_seg = len(SEG_KINDS)
    tm = min(PROJ_TM, t)
    return pl.pallas_call(
        _proj_fused_kernel,
        grid=(t // tm, n_seg),
        in_specs=[pl.BlockSpec((tm, d), lambda i, j: (i, 0)),
                  pl.BlockSpec((d, d), lambda i, j: (0, j)),
                  pl.BlockSpec((None, 1, d), lambda i, j: (j, 0, 0))],
        out_specs=[pl.BlockSpec((tm, d), lambda i, j: (i, j)),
                   pl.BlockSpec((tm, d), lambda i, j: (i, 0))],
        out_shape=[jax.ShapeDtypeStruct((t, n_seg * d), BF16), jax.ShapeDtypeStruct((t, d), F32)],
        compiler_params=_cparams(("parallel", "arbitrary")),
        name="proj_fused",
    )(x_bf, w_cat, aux)


def _fox_forget_kernel(tiles_per_seq, x_ref, w_ref, b_ref, nc_ref, carry_ref):
    i = pl.program_id(0)

    @pl.when(i % tiles_per_seq == 0)
    def _():
        carry_ref[...] = jnp.zeros_like(carry_ref)

    z = _dot(x_ref[...], w_ref[...]) + b_ref[...]
    lf = jnp.minimum(z, 0.0) - jnp.log(1.0 + jnp.exp(-jnp.abs(z)))
    tm = lf.shape[0]
    row = lax.broadcasted_iota(jnp.int32, (tm, tm), 0)
    col = lax.broadcasted_iota(jnp.int32, (tm, tm), 1)
    tri = jnp.where(col <= row, 1.0, 0.0).astype(BF16)
    hi, mid, lo = _split3(lf)
    c = _dot(tri, hi) + _dot(tri, mid) + _dot(tri, lo) + carry_ref[...]
    carry_ref[...] = c[tm - 1:tm, :]
    for h in range(nc_ref.shape[1] // HEAD_DIM):
        nc_ref[:, h * HEAD_DIM:(h + 1) * HEAD_DIM] = jnp.broadcast_to(c[:, h:h + 1] * (-LOG2E), (tm, HEAD_DIM))


def _fox_forget(x_bf, w_pad_bf, b_pad, seq):
    t, d = x_bf.shape
    tm = min(512, seq)
    return pl.pallas_call(
        functools.partial(_fox_forget_kernel, seq // tm),
        grid=(t // tm,),
        in_specs=[pl.BlockSpec((tm, d), lambda i: (i, 0)),
                  pl.BlockSpec((d, LANES), lambda i: (0, 0)),
                  pl.BlockSpec((1, LANES), lambda i: (0, 0))],
        out_specs=pl.BlockSpec((tm, FOX_F_COLS * HEAD_DIM), lambda i: (i, 0)),
        out_shape=jax.ShapeDtypeStruct((t, FOX_F_COLS * HEAD_DIM), F32),
        scratch_shapes=[pltpu.VMEM((1, LANES), F32)],
        compiler_params=_cparams(("arbitrary",)),
        name="fox_forget_cumsum",
    )(x_bf, w_pad_bf, b_pad)


def _bcast_row(a, period, row):
    n, w = a.shape
    a3 = a.reshape(n // period, period, w)
    return jnp.broadcast_to(a3[:, row:row + 1, :], a3.shape).reshape(n, w)


def _hgrn_kernel(q_ref, k_ref, v_ref, g_ref, hg_ref, gain_ref, tri_ref, u_ref, o_ref, st_ref):
    ts = q_ref.shape[0]
    n_chunks = ts // HG_CHUNK

    @pl.when(pl.program_id(2) == 0)
    def _():
        st_ref[...] = jnp.zeros_like(st_ref)

    q = q_ref[...].astype(F32)
    k = k_ref[...].astype(F32)
    v = v_ref[...]
    g = g_ref[...] * LOG2E
    g_hi = g.astype(BF16)
    g_lo = (g - g_hi.astype(F32)).astype(BF16)
    tri = tri_ref[...]
    b = _dot(tri, g_hi) + _dot(tri, g_lo)

    row128 = lax.broadcasted_iota(jnp.int32, (ts, HEAD_DIM), 0)
    row64 = lax.broadcasted_iota(jnp.int32, (ts, HG_CHUNK), 0) % HG_CHUNK
    col64 = lax.broadcasted_iota(jnp.int32, (ts, HG_CHUNK), 1)

    a_key = b - jnp.log2(k)
    parts = []
    for s in range(HG_SUB):
        a_s = _bcast_row(a_key, HG_SUB, s)
        parts.append((q * jnp.exp2(jnp.minimum(b - a_s, 0.0))).astype(BF16))
    scores = _dot(jnp.concatenate(parts, axis=1), u_ref[...])
    same_sub = (row64 // HG_SUB) == (col64 // HG_SUB)
    scores = jnp.where(same_sub & ((col64 % HG_SUB) <= (row64 % HG_SUB)), scores, 0.0)

    level_q, level_k, level_mask = [], [], []
    m = HG_SUB
    while m < HG_CHUNK:
        ref_b = _bcast_row(b, 2 * m, m - 1)
        second = (row128 % (2 * m)) >= m
        e = jnp.exp2(jnp.where(second, b - ref_b, ref_b - b))
        level_q.append((q * e).astype(BF16))
        level_k.append((k * e).astype(BF16))
        level_mask.append(((row64 // (2 * m)) == (col64 // (2 * m)))
                          & ((row64 % (2 * m)) >= m) & ((col64 % (2 * m)) < m))
        m *= 2

    b_last = _bcast_row(b, HG_CHUNK, HG_CHUNK - 1)
    q_in = (q * jnp.exp2(b)).astype(BF16)
    k_out = (k * jnp.exp2(b_last - b)).astype(BF16)

    chunks = [slice(c * HG_CHUNK, (c + 1) * HG_CHUNK) for c in range(n_chunks)]
    intra, state_in = [], []
    for sl in chunks:
        a = scores[sl]
        for lq, lk, lm in zip(level_q, level_k, level_mask):
            a = a + jnp.where(lm[sl], _dot_nt(lq[sl], lk[sl]), 0.0)
        intra.append(_dot(a.astype(BF16), v[sl]))
        state_in.append(_dot_tn(v[sl], k_out[sl]))

    gain = gain_ref[...]
    st = st_ref[...]
    for c, sl in enumerate(chunks):
        o = intra[c] + _dot_nt(q_in[sl], st.astype(BF16))
        st = st * jnp.exp2(b[sl.stop - 1:sl.stop, :]) + state_in[c]
        ms = jnp.mean(o * o, axis=-1, keepdims=True)
        o = o * lax.rsqrt(ms + NORM_EPS) * gain * hg_ref[sl, :].astype(F32)
        o_ref[sl, :] = o.astype(o_ref.dtype)
    st_ref[...] = st


def _hgrn(proj, logf, gain, batch, seq):
    d = logf.shape[-1]
    heads = d // HEAD_DIM
    ts = min(HG_TS, seq)
    shp = (batch, seq, d)
    proj = proj.reshape(batch, seq, proj.shape[-1])
    logf = logf.reshape(shp)
    r = jnp.arange(ts)
    tri = ((r[:, None] // HG_CHUNK == r[None, :] // HG_CHUNK) & (r[None, :] <= r[:, None])).astype(BF16)
    u_rows = jnp.arange(HG_SUB * HEAD_DIM) // HEAD_DIM
    u = (u_rows[:, None] == (jnp.arange(HG_CHUNK)[None, :] % HG_SUB)).astype(BF16)
    tile = pl.BlockSpec((None, ts, HEAD_DIM), lambda b, h, s: (b, s, h))
    seg_tile = lambda seg: pl.BlockSpec((None, ts, HEAD_DIM), lambda b, h, s: (b, s, seg * heads + h))
    out = pl.pallas_call(
        _hgrn_kernel,
        grid=(batch, heads, seq // ts),
        in_specs=[seg_tile(SEG_Q_A), seg_tile(SEG_K_A), seg_tile(SEG_V_A), tile, seg_tile(SEG_HG),
                  pl.BlockSpec((1, HEAD_DIM), lambda b, h, s: (0, 0)),
                  pl.BlockSpec((ts, ts), lambda b, h, s: (0, 0)),
                  pl.BlockSpec((HG_SUB * HEAD_DIM, HG_CHUNK), lambda b, h, s: (0, 0))],
        out_specs=tile,
        out_shape=jax.ShapeDtypeStruct(shp, BF16),
        scratch_shapes=[pltpu.VMEM((HEAD_DIM, HEAD_DIM), F32)],
        compiler_params=_cparams(("parallel", "parallel", "arbitrary")),
        name="hgrn2",
    )(proj, proj, proj, logf, proj, gain, tri, u)
    return out.reshape(batch * seq, d)


def _fox_kernel(qt_ref, k_ref, vt_ref, nc_ref, o_ref, acc_ref, sc_ref):
    tq = qt_ref.shape[1]
    tk = vt_ref.shape[2]
    n_streams = acc_ref.shape[0]
    qi = pl.program_id(2)
    cw = min(FOX_CW, tq)
    n_cg = tq // cw
    acc_ref[...] = jnp.zeros_like(acc_ref)

    chains = [(s, cg) for s in range(n_streams) for cg in range(n_cg)]

    def block_starts(g):
        return [pl.multiple_of((g * n_streams + s) * tk, tk) for s in range(n_streams)]

    def score_matmuls(g):
        starts = block_starts(g)
        for ci, (s, cg) in enumerate(chains):
            sc_ref[g % 2, ci] = _dot(k_ref[pl.ds(starts[s], tk), :], qt_ref[:, cg * cw:(cg + 1) * cw])

    def group(g, carry, masked):
        starts = block_starts(g)
        m_all, probs = [], []
        for ci, ((s, cg), m_old) in enumerate(zip(chains, carry)):
            if masked and s * tk >= (cg + 1) * cw:
                m_all.append(m_old)
                probs.append(None)
                continue
            st = sc_ref[g % 2, ci]
            st = st + jnp.concatenate([nc_ref[pl.ds(starts[s], tk), :]] * (cw // HEAD_DIM), axis=1)
            if masked:
                key_pos = starts[s] + lax.broadcasted_iota(jnp.int32, st.shape, 0)
                query_pos = qi * tq + cg * cw + lax.broadcasted_iota(jnp.int32, st.shape, 1)
                st = jnp.where(key_pos <= query_pos, st, FOX_MASKED)
            m_new = jnp.maximum(m_old, jnp.max(st, axis=0, keepdims=True))
            m_all.append(m_new)
            probs.append((jnp.exp2(m_old - m_new), jnp.exp2((st - m_new).astype(BF16))))
        if not masked:
            score_matmuls(g + 1)
        for (s, cg), ap in zip(chains, probs):
            if ap is not None:
                cols = slice(cg * cw, (cg + 1) * cw)
                acc_ref[s, :, cols] = acc_ref[s, :, cols] * ap[0] + _dot(vt_ref[g * n_streams + s], ap[1])
        return tuple(m_all)

    init = (jnp.full((1, cw), FOX_MASKED, F32),) * len(chains)

    score_matmuls(0)
    carry = lax.fori_loop(0, qi, lambda g, c: group(g, c, False), init)
    carry = group(qi, carry, True)
    carry = tuple(tuple(carry[s * n_cg + cg] for cg in range(n_cg)) for s in range(n_streams))

    for cg in range(n_cg):
        cols = slice(cg * cw, (cg + 1) * cw)
        m = carry[0][cg]
        for s in range(1, n_streams):
            m = jnp.maximum(m, carry[s][cg])
        acc = jnp.zeros((acc_ref.shape[1], cw), F32)
        for s in range(n_streams):
            acc = acc + jnp.exp2(carry[s][cg] - m) * acc_ref[s, :, cols]
        out = acc[:HEAD_DIM] * (1.0 / acc[HEAD_DIM:HEAD_DIM + 1])
        o_ref[cols, :] = out.T.astype(o_ref.dtype)


def _fox(proj, nc_rep, batch, seq):
    d = nc_rep.shape[-1]
    heads = d // HEAD_DIM
    tq = min(FOX_TQ, seq)
    tk = min(FOX_TK, seq)
    shp = (batch, seq, d)
    q = proj[:, SEG_Q_B * d:(SEG_Q_B + 1) * d]
    v = proj[:, SEG_V_B * d:(SEG_V_B + 1) * d]
    qt = q.reshape(batch, seq, heads, HEAD_DIM).transpose(0, 2, 3, 1)
    vt = v.reshape(batch, seq // tk, tk, heads, HEAD_DIM).transpose(0, 3, 1, 4, 2)
    extra = jnp.zeros(vt.shape[:3] + (FOX_VPAD, tk), BF16).at[:, :, :, 0, :].set(1.0)
    vt = jnp.concatenate([vt, extra], axis=3)
    vrows = HEAD_DIM + FOX_VPAD
    kvfull = pl.BlockSpec((None, seq, HEAD_DIM), lambda b, h, i: (b, 0, h))
    out = pl.pallas_call(
        _fox_kernel,
        grid=(batch, heads, seq // tq),
        in_specs=[pl.BlockSpec((None, None, HEAD_DIM, tq), lambda b, h, i: (b, h, 0, i)),
                  pl.BlockSpec((None, seq, HEAD_DIM), lambda b, h, i: (b, 0, SEG_K_B * heads + h)),
                  pl.BlockSpec((None, None, seq // tk, vrows, tk), lambda b, h, i: (b, h, 0, 0, 0)),
                  kvfull],
        out_specs=pl.BlockSpec((None, tq, HEAD_DIM), lambda b, h, i: (b, i, h)),
        out_shape=jax.ShapeDtypeStruct(shp, BF16),
        scratch_shapes=[pltpu.VMEM((tq // tk, vrows, tq), F32),
                        pltpu.VMEM((2, (tq // tk) * (tq // min(FOX_CW, tq)), tk, min(FOX_CW, tq)), F32)],
        compiler_params=_cparams(("parallel", "parallel", "arbitrary")),
        name="fox_attention",
    )(qt, proj.reshape(batch, seq, proj.shape[-1]), vt, nc_rep.reshape(shp))
    return out.reshape(batch * seq, d)


def _layer_norm(z, g, b):
    mu = jnp.mean(z, axis=-1, keepdims=True)
    zc = z - mu
    var = jnp.mean(zc * zc, axis=-1, keepdims=True)
    return zc * lax.rsqrt(var + LN_EPS) * g + b


def _first_index_of_max(vals, lane):
    vmax = jnp.max(vals, axis=-1, keepdims=True)
    idx = jnp.min(jnp.where(vals == vmax, lane, LANES), axis=-1, keepdims=True)
    return vmax, idx


def _merge_kernel(alpha, oa_ref, ob_ref, ga_ref, gb_ref, x_ref, wa_ref, wb_ref, wo_ref, g_ref, b_ref,
                  wr_hi_ref, wr_lo_ref, br_ref, x1_ref, gate_ref, idx_ref):
    y_a = _dot(oa_ref[...], wa_ref[...])
    y_b = _dot(ob_ref[...], wb_ref[...])
    merged = ga_ref[...].astype(F32) * y_a + gb_ref[...].astype(F32) * y_b
    h = _dot(merged.astype(BF16), wo_ref[...])
    x1 = _layer_norm(alpha * x_ref[...] + h, g_ref[...], b_ref[...])
    x1_ref[...] = x1

    x_hi = x1.astype(BF16)
    x_lo = (x1 - x_hi.astype(F32)).astype(BF16)
    wr_hi = wr_hi_ref[...]
    logits = _dot(x_hi, wr_hi) + _dot(x_lo, wr_hi) + _dot(x_hi, wr_lo_ref[...]) + br_ref[...]
    lane = lax.broadcasted_iota(jnp.int32, logits.shape, 1)
    neg = -jnp.inf
    g_logits = jnp.where(lane < N_GROUPS, logits, neg)
    g_max, grp = _first_index_of_max(g_logits, lane)
    p_grp = 1.0 / jnp.sum(jnp.exp(g_logits - g_max), axis=-1, keepdims=True)
    lo = N_GROUPS + EXPERTS_PER_GROUP * grp
    e_logits = jnp.where((lane >= lo) & (lane < lo + EXPERTS_PER_GROUP), logits, neg)
    v1, i1 = _first_index_of_max(e_logits, lane)
    v2, i2 = _first_index_of_max(jnp.where(lane == i1, neg, e_logits), lane)
    e2 = jnp.exp(v2 - v1)
    w1 = 1.0 / (1.0 + e2)
    gate_ref[...] = jnp.where(lane == 0, w1 * p_grp, jnp.where(lane == 1, e2 * w1 * p_grp, 0.0))
    idx_ref[...] = jnp.where(lane == 0, i1 - N_GROUPS, jnp.where(lane == 1, i2 - N_GROUPS, 0))


def _merge(alpha, o_a, o_b, proj, x, wa, wb, wo, ln_g, ln_b, wr_hi, wr_lo, br):
    t, d = x.shape
    tm = min(MERGE_TM, t)
    row = lambda n: pl.BlockSpec((tm, n), lambda i: (i, 0))
    seg = lambda s: pl.BlockSpec((tm, d), lambda i: (i, s))
    full = lambda a: pl.BlockSpec(a.shape, lambda i: (0, 0))
    return pl.pallas_call(
        functools.partial(_merge_kernel, alpha),
        grid=(t // tm,),
        in_specs=[row(d), row(d), seg(SEG_G_A), seg(SEG_G_B), row(d), full(wa), full(wb), full(wo), full(ln_g),
                  full(ln_b), full(wr_hi), full(wr_lo), full(br)],
        out_specs=[row(d), row(LANES), row(LANES)],
        out_shape=[jax.ShapeDtypeStruct((t, d), F32), jax.ShapeDtypeStruct((t, LANES), F32),
                   jax.ShapeDtypeStruct((t, LANES), jnp.int32)],
        compiler_params=_cparams(("parallel",)),
        name="merge_ln_router",
    )(o_a, o_b, proj, proj, x, wa, wb, wo, ln_g, ln_b, wr_hi, wr_lo, br)


def _moe_kernel(blk_e_ref, src_ref, src_next_ref, dst_ref, dst_prev_ref, x_hbm, wg_ref, wu_ref, wd_ref, y_hbm,
                xbuf, ybuf, wg_bf, wu_bf, wd_bf, gsem, ssem):
    i = pl.program_id(0)
    nb = pl.num_programs(0)
    blk = xbuf.shape[1]
    slot = i % 2
    other = 1 - slot

    def gather_row(idx_ref, s, r):
        return pltpu.make_async_copy(x_hbm.at[pl.ds(idx_ref[0, r], 1), :], xbuf.at[s, pl.ds(r, 1), :], gsem.at[s])

    def scatter_row(idx_ref, s, r):
        return pltpu.make_async_copy(ybuf.at[s, pl.ds(r, 1), :], y_hbm.at[pl.ds(idx_ref[0, r], 1), :], ssem.at[s])

    def for_rows(fn):
        def body(r, carry):
            fn(r)
            return carry
        lax.fori_loop(0, blk, body, 0, unroll=8)

    def wait_gather(s):
        pltpu.make_async_copy(x_hbm.at[pl.ds(0, blk)], xbuf.at[s], gsem.at[s]).wait()

    def wait_scatter(s):
        pltpu.make_async_copy(ybuf.at[s], y_hbm.at[pl.ds(0, blk)], ssem.at[s]).wait()

    @pl.when(i == 0)
    def _():
        for_rows(lambda r: gather_row(src_ref, 0, r).start())
        ybuf[1] = jnp.zeros(ybuf.shape[1:], F32)

    @pl.when((i == 0) | (blk_e_ref[i] != blk_e_ref[jnp.maximum(i - 1, 0)]))
    def _():
        wg_bf[...] = wg_ref[...].astype(BF16)
        wu_bf[...] = wu_ref[...].astype(BF16)
        wd_bf[...] = wd_ref[...].astype(BF16)

    wait_gather(slot)

    @pl.when(i >= 1)
    def _():
        wait_scatter(slot)

    xb = xbuf[slot].astype(BF16)
    for r in range(blk):
        gather_row(src_next_ref, other, r).start()
        scatter_row(dst_prev_ref, other, r).start()
    gate = _dot(xb, wg_bf[...])
    up = _dot(xb, wu_bf[...])
    hidden = (gate * jax.nn.sigmoid(gate) * up).astype(BF16)
    ybuf[slot] = _dot(hidden, wd_bf[...])

    @pl.when(i == nb - 1)
    def _():
        for_rows(lambda r: scatter_row(dst_ref, slot, r).start())
        wait_scatter(other)
        wait_scatter(slot)
        wait_gather(other)


def _moe(x1, blk_e, src, dst, wg, wu, wd, layer):
    t, d = x1.shape
    n_blocks = blk_e.shape[0]
    blk = src.shape[-1]
    de = wg.shape[-1]
    smem_idx = lambda fn: pl.BlockSpec((None, 1, blk), fn, memory_space=pltpu.SMEM)
    cur = lambda i, be: (i, 0, 0)
    nxt = lambda i, be: (jnp.minimum(i + 1, n_blocks - 1), 0, 0)
    prv = lambda i, be: ((i + n_blocks - 1) % n_blocks, 0, 0)
    expert = lambda i, be: (layer, be[i], 0, 0)
    grid_spec = pltpu.PrefetchScalarGridSpec(
        num_scalar_prefetch=1,
        grid=(n_blocks,),
        in_specs=[smem_idx(cur), smem_idx(nxt), smem_idx(cur), smem_idx(prv),
                  pl.BlockSpec(memory_space=pl.ANY),
                  pl.BlockSpec((None, None, d, de), expert),
                  pl.BlockSpec((None, None, d, de), expert),
                  pl.BlockSpec((None, None, de, d), expert)],
        out_specs=pl.BlockSpec(memory_space=pl.ANY),
        scratch_shapes=[pltpu.VMEM((2, blk, d), F32), pltpu.VMEM((2, blk, d), F32),
                        pltpu.VMEM((d, de), BF16), pltpu.VMEM((d, de), BF16), pltpu.VMEM((de, d), BF16),
                        pltpu.SemaphoreType.DMA((2,)), pltpu.SemaphoreType.DMA((2,))],
    )
    return pl.pallas_call(
        _moe_kernel,
        grid_spec=grid_spec,
        out_shape=jax.ShapeDtypeStruct((n_blocks * blk, d), F32),
        compiler_params=_cparams(("arbitrary",)),
        name="moe_experts",
    )(blk_e, src, src, dst, dst, x1, wg, wu, wd)


def _moe_plan(expert_idx, t):
    a = t * TOP_K
    blk = min(MOE_BLK, a)
    n_blocks = a // blk + N_EXPERTS
    n_slots = n_blocks * blk
    flat_e = expert_idx.reshape(a)
    order = jnp.argsort(flat_e).astype(jnp.int32)
    counts = jnp.bincount(flat_e, length=N_EXPERTS).astype(jnp.int32)
    padded = (counts + blk - 1) // blk * blk
    pad_end = jnp.cumsum(padded)
    pad_start = pad_end - padded
    start = jnp.cumsum(counts) - counts
    n_used = (pad_end[-1] // blk).astype(jnp.int32)
    block_start = jnp.arange(n_blocks, dtype=jnp.int32) * blk
    blk_e = jnp.sum(block_start[:, None] >= pad_end[None, :], axis=1).astype(jnp.int32)
    blk_e = jnp.minimum(blk_e, N_EXPERTS - 1)
    blk_e = jnp.where(jnp.arange(n_blocks) < n_used, blk_e, blk_e[jnp.maximum(n_used - 1, 0)])
    slot = jnp.arange(n_slots, dtype=jnp.int32)
    slot_e = jnp.repeat(blk_e, blk)
    r = slot - pad_start[slot_e]
    valid = (r < counts[slot_e]) & (slot < pad_end[-1])
    assign = order[jnp.clip(start[slot_e] + r, 0, a - 1)]
    src = jnp.where(valid, assign // TOP_K, 0)
    pad_rank = jnp.cumsum(jnp.where(valid, 0, 1)) - 1
    dst = jnp.where(valid, (assign % TOP_K) * t + assign // TOP_K,
                    jnp.where(slot >= pad_end[-1], slot, a + pad_rank))
    return (blk_e, src.reshape(n_blocks, 1, blk).astype(jnp.int32),
            dst.reshape(n_blocks, 1, blk).astype(jnp.int32))


def _combine_kernel(alpha, x_ref, y0_ref, y1_ref, gate_ref, g_ref, b_ref, o_ref, obf_ref):
    gate = gate_ref[...]
    f = gate[:, 0:1] * y0_ref[...] + gate[:, 1:2] * y1_ref[...]
    out = _layer_norm(alpha * x_ref[...] + f, g_ref[...], b_ref[...])
    o_ref[...] = out
    obf_ref[...] = out.astype(BF16)


def _combine(alpha, x1, y, gate, ln_g, ln_b):
    t, d = x1.shape
    tm = min(COMB_TM, t)
    nt = t // tm
    row = lambda n: pl.BlockSpec((tm, n), lambda i: (i, 0))
    full = lambda a: pl.BlockSpec(a.shape, lambda i: (0, 0))
    return pl.pallas_call(
        functools.partial(_combine_kernel, alpha),
        grid=(nt,),
        in_specs=[row(d), row(d), pl.BlockSpec((tm, d), lambda i: (nt + i, 0)), row(LANES), full(ln_g), full(ln_b)],
        out_specs=[row(d), row(d)],
        out_shape=[jax.ShapeDtypeStruct((t, d), F32), jax.ShapeDtypeStruct((t, d), BF16)],
        compiler_params=_cparams(("parallel",)),
        name="combine_ln",
    )(x1, y, y, gate, ln_g, ln_b)


def kernel(x, w_in, b_fox_f, fox_q_gain, fox_k_gain, hg_lb_logits, hg_norm_gain, w_br_a, w_br_b, w_out, ln1_g, ln1_b, router_g_w, router_g_b, router_e_w, router_e_b, w_gate, w_up, w_down, ln2_g, ln2_b):
    batch, seq, d = x.shape
    depth = w_in.shape[0]
    t = batch * seq
    heads = d // HEAD_DIM
    alpha = (2 * depth) ** 0.25
    assert w_in.shape[2] == 9 * d + FOX_F_COLS

    lb_p = jax.nn.softmax(hg_lb_logits.astype(F32), axis=0)
    lb_all = jnp.cumsum(lb_p, axis=0) - lb_p[0]
    fox_scale = HEAD_DIM ** -0.5 * LOG2E

    xf = x.reshape(t, d)
    xb = xf.astype(BF16)
    for l in range(depth):
        w = w_in[l].astype(BF16)
        w_cat = jnp.concatenate([w[:, :7 * d], w[:, 7 * d + FOX_F_COLS:]], axis=1)
        w_ff = jnp.pad(w[:, 7 * d:7 * d + FOX_F_COLS], ((0, 0), (0, LANES - FOX_F_COLS)))
        b_ff = jnp.pad(b_fox_f[l], (0, LANES - FOX_F_COLS)).reshape(1, LANES)
        aux = jnp.zeros((len(SEG_KINDS), 1, d), F32)
        aux = aux.at[SEG_K_A, 0].set(lb_all[l])
        aux = aux.at[SEG_Q_B, 0].set(jnp.tile(fox_q_gain[l] * fox_scale, heads))
        aux = aux.at[SEG_K_B, 0].set(jnp.tile(fox_k_gain[l], heads))

        proj, logf = _proj_fused(xb, w_cat, aux)
        nc_rep = _fox_forget(xb, w_ff, b_ff, seq)

        o_a = _hgrn(proj, logf, hg_norm_gain[l].reshape(1, HEAD_DIM), batch, seq)

        o_b = _fox(proj, nc_rep, batch, seq)

        w_r = jnp.pad(jnp.concatenate([router_g_w[l], router_e_w[l]], axis=1),
                      ((0, 0), (0, LANES - N_GROUPS - N_EXPERTS)))
        b_r = jnp.pad(jnp.concatenate([router_g_b[l], router_e_b[l]]), (0, LANES - N_GROUPS - N_EXPERTS)).reshape(1, LANES)
        w_r_hi = w_r.astype(BF16)
        w_r_lo = (w_r - w_r_hi.astype(F32)).astype(BF16)
        x1, gate, idx = _merge(alpha, o_a, o_b, proj, xf, w_br_a[l].astype(BF16), w_br_b[l].astype(BF16),
                               w_out[l].astype(BF16), ln1_g[l].reshape(1, d), ln1_b[l].reshape(1, d),
                               w_r_hi, w_r_lo, b_r)

        blk_e, src, dst = _moe_plan(idx[:, :TOP_K], t)
        y = _moe(x1, blk_e, src, dst, w_gate, w_up, w_down, l)
        xf, xb = _combine(alpha, x1, y, gate, ln2_g[l].reshape(1, d), ln2_b[l].reshape(1, d))
    return xf.reshape(batch, seq, d)
```

```python
import functools

import jax
import jax.numpy as jnp
from jax import lax
from jax.experimental import pallas as pl
from jax.experimental.pallas import tpu as pltpu

F32 = jnp.float32
BF16 = jnp.bfloat16

HEAD_DIM = 128
FOX_F_COLS = 8
N_GROUPS = 4
EXPERTS_PER_GROUP = 8
N_EXPERTS = N_GROUPS * EXPERTS_PER_GROUP
TOP_K = 2
LN_EPS = 1e-5
NORM_EPS = 1e-6
LOG2E = 1.4426950408889634
FOX_MASKED = -1e30

LANES = 128
SUBLANES = 8
VMEM_LIMIT_BYTES = 48 * 1024 * 1024

PROJ_TM = 1024
HG_CHUNK = 64
HG_SUB = 8
HG_TS = 256
FOX_TQ = 512
FOX_TK = 256
FOX_CW = 256
FOX_VPAD = 16
MERGE_TM = 512
MOE_BLK = 256
COMB_TM = 512


def _cparams(sem):
    return pltpu.CompilerParams(dimension_semantics=sem, vmem_limit_bytes=VMEM_LIMIT_BYTES)


def _split3(a):
    hi = a.astype(BF16)
    r1 = a - hi.astype(F32)
    mid = r1.astype(BF16)
    lo = (r1 - mid.astype(F32)).astype(BF16)
    return hi, mid, lo


def _dot(a, b):
    return jnp.dot(a, b, preferred_element_type=F32)


def _dot_nt(a, b):
    return lax.dot_general(a, b, (((1,), (1,)), ((), ())), preferred_element_type=F32)


def _dot_tn(a, b):
    return lax.dot_general(a, b, (((0,), (0,)), ((), ())), preferred_element_type=F32)


def _proj_kernel(kind, x_ref, w_ref, *refs):
    acc = _dot(x_ref[...], w_ref[...])
    if kind == "plain":
        (o_ref,) = refs
        o_ref[...] = acc.astype(o_ref.dtype)
    elif kind == "silu":
        (o_ref,) = refs
        o_ref[...] = (acc * jax.nn.sigmoid(acc)).astype(o_ref.dtype)
    elif kind == "sigmoid":
        (o_ref,) = refs
        o_ref[...] = jax.nn.sigmoid(acc).astype(o_ref.dtype)
    elif kind == "forget":
        lb_ref, logf_ref, k_ref = refs
        lb = lb_ref[...]
        f = lb + (1.0 - lb) * jax.nn.sigmoid(acc)
        logf_ref[...] = jnp.log(f)
        k_ref[...] = (1.0 - f).astype(k_ref.dtype)
    elif kind == "rms":
        gain_ref, o_ref = refs
        for h in range(acc.shape[1] // HEAD_DIM):
            sl = slice(h * HEAD_DIM, (h + 1) * HEAD_DIM)
            a = acc[:, sl]
            ms = jnp.mean(a * a, axis=-1, keepdims=True)
            o_ref[:, sl] = (a * lax.rsqrt(ms + NORM_EPS) * gain_ref[:, sl]).astype(o_ref.dtype)
    else:
        raise ValueError(kind)


def _proj(x_bf, w_bf, kind, aux=(), out_dtypes=(BF16,)):
    t, d = x_bf.shape
    n = w_bf.shape[1]
    tm = min(PROJ_TM, t)
    in_specs = [pl.BlockSpec((tm, d), lambda i: (i, 0)), pl.BlockSpec((d, n), lambda i: (0, 0))]
    in_specs += [pl.BlockSpec((1, n), lambda i: (0, 0)) for _ in aux]
    out_specs = [pl.BlockSpec((tm, n), lambda i: (i, 0)) for _ in out_dtypes]
    out_shape = [jax.ShapeDtypeStruct((t, n), dt) for dt in out_dtypes]
    outs = pl.pallas_call(
        functools.partial(_proj_kernel, kind),
        grid=(t // tm,),
        in_specs=in_specs,
        out_specs=out_specs,
        out_shape=out_shape,
        compiler_params=_cparams(("parallel",)),
        name=f"proj_{kind}",
    )(x_bf, w_bf, *aux)
    return outs


def _fox_forget_kernel(tiles_per_seq, x_ref, w_ref, b_ref, nc_ref, carry_ref):
    i = pl.program_id(0)

    @pl.when(i % tiles_per_seq == 0)
    def _():
        carry_ref[...] = jnp.zeros_like(carry_ref)

    z = _dot(x_ref[...], w_ref[...]) + b_ref[...]
    lf = jnp.minimum(z, 0.0) - jnp.log(1.0 + jnp.exp(-jnp.abs(z)))
    tm = lf.shape[0]
    row = lax.broadcasted_iota(jnp.int32, (tm, tm), 0)
    col = lax.broadcasted_iota(jnp.int32, (tm, tm), 1)
    tri = jnp.where(col <= row, 1.0, 0.0).astype(BF16)
    hi, mid, lo = _split3(lf)
    c = _dot(tri, hi) + _dot(tri, mid) + _dot(tri, lo) + carry_ref[...]
    carry_ref[...] = c[tm - 1:tm, :]
    for h in range(nc_ref.shape[1] // HEAD_DIM):
        nc_ref[:, h * HEAD_DIM:(h + 1) * HEAD_DIM] = jnp.broadcast_to(c[:, h:h + 1] * (-LOG2E), (tm, HEAD_DIM))


def _fox_forget(x_bf, w_pad_bf, b_pad, seq):
    t, d = x_bf.shape
    tm = min(512, seq)
    return pl.pallas_call(
        functools.partial(_fox_forget_kernel, seq // tm),
        grid=(t // tm,),
        in_specs=[pl.BlockSpec((tm, d), lambda i: (i, 0)),
                  pl.BlockSpec((d, LANES), lambda i: (0, 0)),
                  pl.BlockSpec((1, LANES), lambda i: (0, 0))],
        out_specs=pl.BlockSpec((tm, FOX_F_COLS * HEAD_DIM), lambda i: (i, 0)),
        out_shape=jax.ShapeDtypeStruct((t, FOX_F_COLS * HEAD_DIM), F32),
        scratch_shapes=[pltpu.VMEM((1, LANES), F32)],
        compiler_params=_cparams(("arbitrary",)),
        name="fox_forget_cumsum",
    )(x_bf, w_pad_bf, b_pad)


def _bcast_row(a, period, row):
    n, w = a.shape
    a3 = a.reshape(n // period, period, w)
    return jnp.broadcast_to(a3[:, row:row + 1, :], a3.shape).reshape(n, w)


def _hgrn_kernel(q_ref, k_ref, v_ref, g_ref, hg_ref, gain_ref, tri_ref, u_ref, o_ref, st_ref):
    ts = q_ref.shape[0]
    n_chunks = ts // HG_CHUNK

    @pl.when(pl.program_id(2) == 0)
    def _():
        st_ref[...] = jnp.zeros_like(st_ref)

    q = q_ref[...].astype(F32)
    k = k_ref[...].astype(F32)
    v = v_ref[...]
    g = g_ref[...] * LOG2E
    g_hi = g.astype(BF16)
    g_lo = (g - g_hi.astype(F32)).astype(BF16)
    tri = tri_ref[...]
    b = _dot(tri, g_hi) + _dot(tri, g_lo)

    row128 = lax.broadcasted_iota(jnp.int32, (ts, HEAD_DIM), 0)
    row64 = lax.broadcasted_iota(jnp.int32, (ts, HG_CHUNK), 0) % HG_CHUNK
    col64 = lax.broadcasted_iota(jnp.int32, (ts, HG_CHUNK), 1)

    a_key = b - jnp.log2(k)
    parts = []
    for s in range(HG_SUB):
        a_s = _bcast_row(a_key, HG_SUB, s)
        parts.append((q * jnp.exp2(jnp.minimum(b - a_s, 0.0))).astype(BF16))
    scores = _dot(jnp.concatenate(parts, axis=1), u_ref[...])
    same_sub = (row64 // HG_SUB) == (col64 // HG_SUB)
    scores = jnp.where(same_sub & ((col64 % HG_SUB) <= (row64 % HG_SUB)), scores, 0.0)

    level_q, level_k, level_mask = [], [], []
    m = HG_SUB
    while m < HG_CHUNK:
        ref_b = _bcast_row(b, 2 * m, m - 1)
        second = (row128 % (2 * m)) >= m
        e = jnp.exp2(jnp.where(second, b - ref_b, ref_b - b))
        level_q.append((q * e).astype(BF16))
        level_k.append((k * e).astype(BF16))
        level_mask.append(((row64 // (2 * m)) == (col64 // (2 * m)))
                          & ((row64 % (2 * m)) >= m) & ((col64 % (2 * m)) < m))
        m *= 2

    b_last = _bcast_row(b, HG_CHUNK, HG_CHUNK - 1)
    q_in = (q * jnp.exp2(b)).astype(BF16)
    k_out = (k * jnp.exp2(b_last - b)).astype(BF16)

    chunks = [slice(c * HG_CHUNK, (c + 1) * HG_CHUNK) for c in range(n_chunks)]
    intra, state_in = [], []
    for sl in chunks:
        a = scores[sl]
        for lq, lk, lm in zip(level_q, level_k, level_mask):
            a = a + jnp.where(lm[sl], _dot_nt(lq[sl], lk[sl]), 0.0)
        intra.append(_dot(a.astype(BF16), v[sl]))
        state_in.append(_dot_tn(v[sl], k_out[sl]))

    gain = gain_ref[...]
    st = st_ref[...]
    for c, sl in enumerate(chunks):
        o = intra[c] + _dot_nt(q_in[sl], st.astype(BF16))
        st = st * jnp.exp2(b[sl.stop - 1:sl.stop, :]) + state_in[c]
        ms = jnp.mean(o * o, axis=-1, keepdims=True)
        o = o * lax.rsqrt(ms + NORM_EPS) * gain * hg_ref[sl, :].astype(F32)
        o_ref[sl, :] = o.astype(o_ref.dtype)
    st_ref[...] = st


def _hgrn(q, k, v, logf, hg_act, gain, batch, seq):
    d = q.shape[-1]
    heads = d // HEAD_DIM
    ts = min(HG_TS, seq)
    shp = (batch, seq, d)
    q, k, v, logf, hg_act = (a.reshape(shp) for a in (q, k, v, logf, hg_act))
    r = jnp.arange(ts)
    tri = ((r[:, None] // HG_CHUNK == r[None, :] // HG_CHUNK) & (r[None, :] <= r[:, None])).astype(BF16)
    u_rows = jnp.arange(HG_SUB * HEAD_DIM) // HEAD_DIM
    u = (u_rows[:, None] == (jnp.arange(HG_CHUNK)[None, :] % HG_SUB)).astype(BF16)
    tile = pl.BlockSpec((None, ts, HEAD_DIM), lambda b, h, s: (b, s, h))
    out = pl.pallas_call(
        _hgrn_kernel,
        grid=(batch, heads, seq // ts),
        in_specs=[tile, tile, tile, tile, tile,
                  pl.BlockSpec((1, HEAD_DIM), lambda b, h, s: (0, 0)),
                  pl.BlockSpec((ts, ts), lambda b, h, s: (0, 0)),
                  pl.BlockSpec((HG_SUB * HEAD_DIM, HG_CHUNK), lambda b, h, s: (0, 0))],
        out_specs=tile,
        out_shape=jax.ShapeDtypeStruct(shp, BF16),
        scratch_shapes=[pltpu.VMEM((HEAD_DIM, HEAD_DIM), F32)],
        compiler_params=_cparams(("parallel", "parallel", "arbitrary")),
        name="hgrn2",
    )(q, k, v, logf, hg_act, gain, tri, u)
    return out.reshape(batch * seq, d)


def _fox_kernel(qt_ref, k_ref, vt_ref, nc_ref, o_ref, acc_ref, sc_ref):
    tq = qt_ref.shape[1]
    tk = vt_ref.shape[2]
    n_streams = acc_ref.shape[0]
    qi = pl.program_id(2)
    cw = min(FOX_CW, tq)
    n_cg = tq // cw
    acc_ref[...] = jnp.zeros_like(acc_ref)

    chains = [(s, cg) for s in range(n_streams) for cg in range(n_cg)]

    def block_starts(g):
        return [pl.multiple_of((g * n_streams + s) * tk, tk) for s in range(n_streams)]

    def score_matmuls(g):
        starts = block_starts(g)
        for ci, (s, cg) in enumerate(chains):
            sc_ref[g % 2, ci] = _dot(k_ref[pl.ds(starts[s], tk), :], qt_ref[:, cg * cw:(cg + 1) * cw])

    def group(g, carry, masked):
        starts = block_starts(g)
        m_all, probs = [], []
        for ci, ((s, cg), m_old) in enumerate(zip(chains, carry)):
            if masked and s * tk >= (cg + 1) * cw:
                m_all.append(m_old)
                probs.append(None)
                continue
            st = sc_ref[g % 2, ci]
            st = st + jnp.concatenate([nc_ref[pl.ds(starts[s], tk), :]] * (cw // HEAD_DIM), axis=1)
            if masked:
                key_pos = starts[s] + lax.broadcasted_iota(jnp.int32, st.shape, 0)
                query_pos = qi * tq + cg * cw + lax.broadcasted_iota(jnp.int32, st.shape, 1)
                st = jnp.where(key_pos <= query_pos, st, FOX_MASKED)
            m_new = jnp.maximum(m_old, jnp.max(st, axis=0, keepdims=True))
            m_all.append(m_new)
            probs.append((jnp.exp2(m_old - m_new), jnp.exp2((st - m_new).astype(BF16))))
        if not masked:
            score_matmuls(g + 1)
        for (s, cg), ap in zip(chains, probs):
            if ap is not None:
                cols = slice(cg * cw, (cg + 1) * cw)
                acc_ref[s, :, cols] = acc_ref[s, :, cols] * ap[0] + _dot(vt_ref[g * n_streams + s], ap[1])
        return tuple(m_all)

    init = (jnp.full((1, cw), FOX_MASKED, F32),) * len(chains)

    score_matmuls(0)
    carry = lax.fori_loop(0, qi, lambda g, c: group(g, c, False), init)
    carry = group(qi, carry, True)
    carry = tuple(tuple(carry[s * n_cg + cg] for cg in range(n_cg)) for s in range(n_streams))

    for cg in range(n_cg):
        cols = slice(cg * cw, (cg + 1) * cw)
        m = carry[0][cg]
        for s in range(1, n_streams):
            m = jnp.maximum(m, carry[s][cg])
        acc = jnp.zeros((acc_ref.shape[1], cw), F32)
        for s in range(n_streams):
            acc = acc + jnp.exp2(carry[s][cg] - m) * acc_ref[s, :, cols]
        out = acc[:HEAD_DIM] * (1.0 / acc[HEAD_DIM:HEAD_DIM + 1])
        o_ref[cols, :] = out.T.astype(o_ref.dtype)


def _fox(q, k, v, nc_rep, batch, seq):
    d = q.shape[-1]
    heads = d // HEAD_DIM
    tq = min(FOX_TQ, seq)
    tk = min(FOX_TK, seq)
    shp = (batch, seq, d)
    qt = q.reshape(batch, seq, heads, HEAD_DIM).transpose(0, 2, 3, 1)
    vt = v.reshape(batch, seq // tk, tk, heads, HEAD_DIM).transpose(0, 3, 1, 4, 2)
    extra = jnp.zeros(vt.shape[:3] + (FOX_VPAD, tk), BF16).at[:, :, :, 0, :].set(1.0)
    vt = jnp.concatenate([vt, extra], axis=3)
    vrows = HEAD_DIM + FOX_VPAD
    kvfull = pl.BlockSpec((None, seq, HEAD_DIM), lambda b, h, i: (b, 0, h))
    out = pl.pallas_call(
        _fox_kernel,
        grid=(batch, heads, seq // tq),
        in_specs=[pl.BlockSpec((None, None, HEAD_DIM, tq), lambda b, h, i: (b, h, 0, i)),
                  kvfull,
                  pl.BlockSpec((None, None, seq // tk, vrows, tk), lambda b, h, i: (b, h, 0, 0, 0)),
                  kvfull],
        out_specs=pl.BlockSpec((None, tq, HEAD_DIM), lambda b, h, i: (b, i, h)),
        out_shape=jax.ShapeDtypeStruct(shp, BF16),
        scratch_shapes=[pltpu.VMEM((tq // tk, vrows, tq), F32),
                        pltpu.VMEM((2, (tq // tk) * (tq // min(FOX_CW, tq)), tk, min(FOX_CW, tq)), F32)],
        compiler_params=_cparams(("parallel", "parallel", "arbitrary")),
        name="fox_attention",
    )(qt, k.reshape(shp), vt, nc_rep.reshape(shp))
    return out.reshape(batch * seq, d)


def _layer_norm(z, g, b):
    mu = jnp.mean(z, axis=-1, keepdims=True)
    zc = z - mu
    var = jnp.mean(zc * zc, axis=-1, keepdims=True)
    return zc * lax.rsqrt(var + LN_EPS) * g + b


def _first_index_of_max(vals, lane):
    vmax = jnp.max(vals, axis=-1, keepdims=True)
    idx = jnp.min(jnp.where(vals == vmax, lane, LANES), axis=-1, keepdims=True)
    return vmax, idx


def _merge_kernel(alpha, oa_ref, ob_ref, ga_ref, gb_ref, x_ref, wa_ref, wb_ref, wo_ref, g_ref, b_ref,
                  wr_hi_ref, wr_lo_ref, br_ref, x1_ref, gate_ref, idx_ref):
    y_a = _dot(oa_ref[...], wa_ref[...])
    y_b = _dot(ob_ref[...], wb_ref[...])
    merged = ga_ref[...].astype(F32) * y_a + gb_ref[...].astype(F32) * y_b
    h = _dot(merged.astype(BF16), wo_ref[...])
    x1 = _layer_norm(alpha * x_ref[...] + h, g_ref[...], b_ref[...])
    x1_ref[...] = x1

    x_hi = x1.astype(BF16)
    x_lo = (x1 - x_hi.astype(F32)).astype(BF16)
    wr_hi = wr_hi_ref[...]
    logits = _dot(x_hi, wr_hi) + _dot(x_lo, wr_hi) + _dot(x_hi, wr_lo_ref[...]) + br_ref[...]
    lane = lax.broadcasted_iota(jnp.int32, logits.shape, 1)
    neg = -jnp.inf
    g_logits = jnp.where(lane < N_GROUPS, logits, neg)
    g_max, grp = _first_index_of_max(g_logits, lane)
    p_grp = 1.0 / jnp.sum(jnp.exp(g_logits - g_max), axis=-1, keepdims=True)
    lo = N_GROUPS + EXPERTS_PER_GROUP * grp
    e_logits = jnp.where((lane >= lo) & (lane < lo + EXPERTS_PER_GROUP), logits, neg)
    v1, i1 = _first_index_of_max(e_logits, lane)
    v2, i2 = _first_index_of_max(jnp.where(lane == i1, neg, e_logits), lane)
    e2 = jnp.exp(v2 - v1)
    w1 = 1.0 / (1.0 + e2)
    gate_ref[...] = jnp.where(lane == 0, w1 * p_grp, jnp.where(lane == 1, e2 * w1 * p_grp, 0.0))
    idx_ref[...] = jnp.where(lane == 0, i1 - N_GROUPS, jnp.where(lane == 1, i2 - N_GROUPS, 0))


def _merge(alpha, o_a, o_b, ga, gb, x, wa, wb, wo, ln_g, ln_b, wr_hi, wr_lo, br):
    t, d = x.shape
    tm = min(MERGE_TM, t)
    row = lambda n: pl.BlockSpec((tm, n), lambda i: (i, 0))
    full = lambda a: pl.BlockSpec(a.shape, lambda i: (0, 0))
    return pl.pallas_call(
        functools.partial(_merge_kernel, alpha),
        grid=(t // tm,),
        in_specs=[row(d), row(d), row(d), row(d), row(d), full(wa), full(wb), full(wo), full(ln_g), full(ln_b),
                  full(wr_hi), full(wr_lo), full(br)],
        out_specs=[row(d), row(LANES), row(LANES)],
        out_shape=[jax.ShapeDtypeStruct((t, d), F32), jax.ShapeDtypeStruct((t, LANES), F32),
                   jax.ShapeDtypeStruct((t, LANES), jnp.int32)],
        compiler_params=_cparams(("parallel",)),
        name="merge_ln_router",
    )(o_a, o_b, ga, gb, x, wa, wb, wo, ln_g, ln_b, wr_hi, wr_lo, br)


def _moe_kernel(blk_e_ref, src_ref, src_next_ref, dst_ref, dst_prev_ref, x_hbm, wg_ref, wu_ref, wd_ref, y_hbm,
                xbuf, ybuf, wg_bf, wu_bf, wd_bf, gsem, ssem):
    i = pl.program_id(0)
    nb = pl.num_programs(0)
    blk = xbuf.shape[1]
    slot = i % 2
    other = 1 - slot

    def gather_row(idx_ref, s, r):
        return pltpu.make_async_copy(x_hbm.at[pl.ds(idx_ref[0, r], 1), :], xbuf.at[s, pl.ds(r, 1), :], gsem.at[s])

    def scatter_row(idx_ref, s, r):
        return pltpu.make_async_copy(ybuf.at[s, pl.ds(r, 1), :], y_hbm.at[pl.ds(idx_ref[0, r], 1), :], ssem.at[s])

    def for_rows(fn):
        def body(r, carry):
            fn(r)
            return carry
        lax.fori_loop(0, blk, body, 0, unroll=8)

    def wait_gather(s):
        pltpu.make_async_copy(x_hbm.at[pl.ds(0, blk), :], xbuf.at[s], gsem.at[s]).wait()

    def wait_scatter(s):
        pltpu.make_async_copy(ybuf.at[s], y_hbm.at[pl.ds(0, blk), :], ssem.at[s]).wait()

    @pl.when(i == 0)
    def _():
        for_rows(lambda r: gather_row(src_ref, 0, r).start())
        ybuf[1] = jnp.zeros((blk, ybuf.shape[2]), F32)

    @pl.when((i == 0) | (blk_e_ref[i] != blk_e_ref[jnp.maximum(i - 1, 0)]))
    def _():
        wg_bf[...] = wg_ref[...].astype(BF16)
        wu_bf[...] = wu_ref[...].astype(BF16)
        wd_bf[...] = wd_ref[...].astype(BF16)

    wait_gather(slot)

    @pl.when(i >= 1)
    def _():
        wait_scatter(slot)

    xb = xbuf[slot].astype(BF16)
    for r in range(blk):
        gather_row(src_next_ref, other, r).start()
        scatter_row(dst_prev_ref, other, r).start()
    gate = _dot(xb, wg_bf[...])
    up = _dot(xb, wu_bf[...])
    hidden = (gate * jax.nn.sigmoid(gate) * up).astype(BF16)
    ybuf[slot] = _dot(hidden, wd_bf[...])

    @pl.when(i == nb - 1)
    def _():
        for_rows(lambda r: scatter_row(dst_ref, slot, r).start())
        wait_scatter(other)
        wait_scatter(slot)
        wait_gather(other)


def _moe(x1, blk_e, src, dst, wg, wu, wd, layer):
    t, d = x1.shape
    n_blocks = blk_e.shape[0]
    blk = src.shape[-1]
    de = wg.shape[-1]
    smem_idx = lambda fn: pl.BlockSpec((None, 1, blk), fn, memory_space=pltpu.SMEM)
    cur = lambda i, be: (i, 0, 0)
    nxt = lambda i, be: (jnp.minimum(i + 1, n_blocks - 1), 0, 0)
    prv = lambda i, be: ((i + n_blocks - 1) % n_blocks, 0, 0)
    expert = lambda i, be: (layer, be[i], 0, 0)
    grid_spec = pltpu.PrefetchScalarGridSpec(
        num_scalar_prefetch=1,
        grid=(n_blocks,),
        in_specs=[smem_idx(cur), smem_idx(nxt), smem_idx(cur), smem_idx(prv),
                  pl.BlockSpec(memory_space=pl.ANY),
                  pl.BlockSpec((None, None, d, de), expert),
                  pl.BlockSpec((None, None, d, de), expert),
                  pl.BlockSpec((None, None, de, d), expert)],
        out_specs=pl.BlockSpec(memory_space=pl.ANY),
        scratch_shapes=[pltpu.VMEM((2, blk, d), F32), pltpu.VMEM((2, blk, d), F32),
                        pltpu.VMEM((d, de), BF16), pltpu.VMEM((d, de), BF16), pltpu.VMEM((de, d), BF16),
                        pltpu.SemaphoreType.DMA((2,)), pltpu.SemaphoreType.DMA((2,))],
    )
    return pl.pallas_call(
        _moe_kernel,
        grid_spec=grid_spec,
        out_shape=jax.ShapeDtypeStruct((n_blocks * blk, d), F32),
        compiler_params=_cparams(("arbitrary",)),
        name="moe_experts",
    )(blk_e, src, src, dst, dst, x1, wg, wu, wd)


def _moe_plan(expert_idx, t):
    a = t * TOP_K
    blk = min(MOE_BLK, a)
    n_blocks = a // blk + N_EXPERTS
    n_slots = n_blocks * blk
    flat_e = expert_idx.reshape(a)
    order = jnp.argsort(flat_e).astype(jnp.int32)
    counts = jnp.bincount(flat_e, length=N_EXPERTS).astype(jnp.int32)
    padded = (counts + blk - 1) // blk * blk
    pad_end = jnp.cumsum(padded)
    pad_start = pad_end - padded
    start = jnp.cumsum(counts) - counts
    n_used = (pad_end[-1] // blk).astype(jnp.int32)
    block_start = jnp.arange(n_blocks, dtype=jnp.int32) * blk
    blk_e = jnp.sum(block_start[:, None] >= pad_end[None, :], axis=1).astype(jnp.int32)
    blk_e = jnp.minimum(blk_e, N_EXPERTS - 1)
    blk_e = jnp.where(jnp.arange(n_blocks) < n_used, blk_e, blk_e[jnp.maximum(n_used - 1, 0)])
    slot = jnp.arange(n_slots, dtype=jnp.int32)
    slot_e = jnp.repeat(blk_e, blk)
    r = slot - pad_start[slot_e]
    valid = (r < counts[slot_e]) & (slot < pad_end[-1])
    assign = order[jnp.clip(start[slot_e] + r, 0, a - 1)]
    src = jnp.where(valid, assign // TOP_K, 0)
    pad_rank = jnp.cumsum(jnp.where(valid, 0, 1)) - 1
    dst = jnp.where(valid, (assign % TOP_K) * t + assign // TOP_K,
                    jnp.where(slot >= pad_end[-1], slot, a + pad_rank))
    return (blk_e, src.reshape(n_blocks, 1, blk).astype(jnp.int32),
            dst.reshape(n_blocks, 1, blk).astype(jnp.int32))


def _combine_kernel(alpha, x_ref, y0_ref, y1_ref, gate_ref, g_ref, b_ref, o_ref, obf_ref):
    gate = gate_ref[...]
    f = gate[:, 0:1] * y0_ref[...] + gate[:, 1:2] * y1_ref[...]
    out = _layer_norm(alpha * x_ref[...] + f, g_ref[...], b_ref[...])
    o_ref[...] = out
    obf_ref[...] = out.astype(BF16)


def _combine(alpha, x1, y, gate, ln_g, ln_b):
    t, d = x1.shape
    tm = min(COMB_TM, t)
    nt = t // tm
    row = lambda n: pl.BlockSpec((tm, n), lambda i: (i, 0))
    full = lambda a: pl.BlockSpec(a.shape, lambda i: (0, 0))
    return pl.pallas_call(
        functools.partial(_combine_kernel, alpha),
        grid=(nt,),
        in_specs=[row(d), row(d), pl.BlockSpec((tm, d), lambda i: (nt + i, 0)), row(LANES), full(ln_g), full(ln_b)],
        out_specs=[row(d), row(d)],
        out_shape=[jax.ShapeDtypeStruct((t, d), F32), jax.ShapeDtypeStruct((t, d), BF16)],
        compiler_params=_cparams(("parallel",)),
        name="combine_ln",
    )(x1, y, y, gate, ln_g, ln_b)


def kernel(x, w_in, b_fox_f, fox_q_gain, fox_k_gain, hg_lb_logits, hg_norm_gain, w_br_a, w_br_b, w_out, ln1_g, ln1_b, router_g_w, router_g_b, router_e_w, router_e_b, w_gate, w_up, w_down, ln2_g, ln2_b):
    batch, seq, d = x.shape
    depth = w_in.shape[0]
    t = batch * seq
    heads = d // HEAD_DIM
    alpha = (2 * depth) ** 0.25
    assert w_in.shape[2] == 9 * d + FOX_F_COLS

    lb_p = jax.nn.softmax(hg_lb_logits.astype(F32), axis=0)
    lb_all = jnp.cumsum(lb_p, axis=0) - lb_p[0]
    fox_scale = HEAD_DIM ** -0.5 * LOG2E

    xf = x.reshape(t, d)
    xb = xf.astype(BF16)
    for l in range(depth):
        w = w_in[l].astype(BF16)
        col = lambda j: w[:, j * d:(j + 1) * d]
        w_gates = w[:, 7 * d + FOX_F_COLS:]
        w_ff = jnp.pad(w[:, 7 * d:7 * d + FOX_F_COLS], ((0, 0), (0, LANES - FOX_F_COLS)))
        b_ff = jnp.pad(b_fox_f[l], (0, LANES - FOX_F_COLS)).reshape(1, LANES)

        (q_a,) = _proj(xb, col(0), "silu")
        logf, k_a = _proj(xb, col(1), "forget", aux=(lb_all[l].reshape(1, d),), out_dtypes=(F32, BF16))
        (v_a,) = _proj(xb, col(2), "plain")
        (hg_act,) = _proj(xb, col(3), "silu")
        (q_b,) = _proj(xb, col(4), "rms", aux=(jnp.tile(fox_q_gain[l] * fox_scale, heads).reshape(1, d),))
        (k_b,) = _proj(xb, col(5), "rms", aux=(jnp.tile(fox_k_gain[l], heads).reshape(1, d),))
        (v_b,) = _proj(xb, col(6), "plain")
        (g_a,) = _proj(xb, w_gates[:, :d], "sigmoid")
        (g_b,) = _proj(xb, w_gates[:, d:], "sigmoid")
        nc_rep = _fox_forget(xb, w_ff, b_ff, seq)

        o_a = _hgrn(q_a, k_a, v_a, logf, hg_act, hg_norm_gain[l].reshape(1, HEAD_DIM), batch, seq)

        o_b = _fox(q_b, k_b, v_b, nc_rep, batch, seq)

        w_r = jnp.pad(jnp.concatenate([router_g_w[l], router_e_w[l]], axis=1),
                      ((0, 0), (0, LANES - N_GROUPS - N_EXPERTS)))
        b_r = jnp.pad(jnp.concatenate([router_g_b[l], router_e_b[l]]), (0, LANES - N_GROUPS - N_EXPERTS)).reshape(1, LANES)
        w_r_hi = w_r.astype(BF16)
        w_r_lo = (w_r - w_r_hi.astype(F32)).astype(BF16)
        x1, gate, idx = _merge(alpha, o_a, o_b, g_a, g_b, xf, w_br_a[l].astype(BF16), w_br_b[l].astype(BF16),
                               w_out[l].astype(BF16), ln1_g[l].reshape(1, d), ln1_b[l].reshape(1, d),
                               w_r_hi, w_r_lo, b_r)

        blk_e, src, dst = _moe_plan(idx[:, :TOP_K], t)
        y = _moe(x1, blk_e, src, dst, w_gate, w_up, w_down, l)
        xf, xb = _combine(alpha, x1, y, gate, ln2_g[l].reshape(1, d), ln2_b[l].reshape(1, d))
    return xf.reshape(batch, seq, d)
```

```python
import functools

import jax
import jax.numpy as jnp
from jax import lax
from jax.experimental import pallas as pl
from jax.experimental.pallas import tpu as pltpu

F32 = jnp.float32
BF16 = jnp.bfloat16

HEAD_DIM = 128
FOX_F_COLS = 8
N_GROUPS = 4
EXPERTS_PER_GROUP = 8
N_EXPERTS = N_GROUPS * EXPERTS_PER_GROUP
TOP_K = 2
LN_EPS = 1e-5
NORM_EPS = 1e-6
LOG2E = 1.4426950408889634
FOX_MASKED = -1e30

LANES = 128
SUBLANES = 8
VMEM_LIMIT_BYTES = 48 * 1024 * 1024

PROJ_TM = 1024
HG_CHUNK = 64
HG_SUB = 8
HG_TS = 256
FOX_TQ = 512
FOX_TK = 256
FOX_CW = 256
FOX_VPAD = 16
MERGE_TM = 512
MOE_BLK = 256
COMB_TM = 512


def _cparams(sem):
    return pltpu.CompilerParams(dimension_semantics=sem, vmem_limit_bytes=VMEM_LIMIT_BYTES)


def _split3(a):
    hi = a.astype(BF16)
    r1 = a - hi.astype(F32)
    mid = r1.astype(BF16)
    lo = (r1 - mid.astype(F32)).astype(BF16)
    return hi, mid, lo


def _dot(a, b):
    return jnp.dot(a, b, preferred_element_type=F32)


def _dot_nt(a, b):
    return lax.dot_general(a, b, (((1,), (1,)), ((), ())), preferred_element_type=F32)


def _dot_tn(a, b):
    return lax.dot_general(a, b, (((0,), (0,)), ((), ())), preferred_element_type=F32)


def _proj_kernel(kind, x_ref, w_ref, *refs):
    acc = _dot(x_ref[...], w_ref[...])
    if kind == "plain":
        (o_ref,) = refs
        o_ref[...] = acc.astype(o_ref.dtype)
    elif kind == "silu":
        (o_ref,) = refs
        o_ref[...] = (acc * jax.nn.sigmoid(acc)).astype(o_ref.dtype)
    elif kind == "sigmoid":
        (o_ref,) = refs
        o_ref[...] = jax.nn.sigmoid(acc).astype(o_ref.dtype)
    elif kind == "forget":
        lb_ref, logf_ref, k_ref = refs
        lb = lb_ref[...]
        f = lb + (1.0 - lb) * jax.nn.sigmoid(acc)
        logf_ref[...] = jnp.log(f)
        k_ref[...] = (1.0 - f).astype(k_ref.dtype)
    elif kind == "rms":
        gain_ref, o_ref = refs
        for h in range(acc.shape[1] // HEAD_DIM):
            sl = slice(h * HEAD_DIM, (h + 1) * HEAD_DIM)
            a = acc[:, sl]
            ms = jnp.mean(a * a, axis=-1, keepdims=True)
            o_ref[:, sl] = (a * lax.rsqrt(ms + NORM_EPS) * gain_ref[:, sl]).astype(o_ref.dtype)
    else:
        raise ValueError(kind)


def _proj(x_bf, w_bf, kind, aux=(), out_dtypes=(BF16,)):
    t, d = x_bf.shape
    n = w_bf.shape[1]
    tm = min(PROJ_TM, t)
    in_specs = [pl.BlockSpec((tm, d), lambda i: (i, 0)), pl.BlockSpec((d, n), lambda i: (0, 0))]
    in_specs += [pl.BlockSpec((1, n), lambda i: (0, 0)) for _ in aux]
    out_specs = [pl.BlockSpec((tm, n), lambda i: (i, 0)) for _ in out_dtypes]
    out_shape = [jax.ShapeDtypeStruct((t, n), dt) for dt in out_dtypes]
    outs = pl.pallas_call(
        functools.partial(_proj_kernel, kind),
        grid=(t // tm,),
        in_specs=in_specs,
        out_specs=out_specs,
        out_shape=out_shape,
        compiler_params=_cparams(("parallel",)),
        name=f"proj_{kind}",
    )(x_bf, w_bf, *aux)
    return outs


def _fox_forget_kernel(tiles_per_seq, x_ref, w_ref, b_ref, nc_ref, carry_ref):
    i = pl.program_id(0)

    @pl.when(i % tiles_per_seq == 0)
    def _():
        carry_ref[...] = jnp.zeros_like(carry_ref)

    z = _dot(x_ref[...], w_ref[...]) + b_ref[...]
    lf = jnp.minimum(z, 0.0) - jnp.log(1.0 + jnp.exp(-jnp.abs(z)))
    tm = lf.shape[0]
    row = lax.broadcasted_iota(jnp.int32, (tm, tm), 0)
    col = lax.broadcasted_iota(jnp.int32, (tm, tm), 1)
    tri = jnp.where(col <= row, 1.0, 0.0).astype(BF16)
    hi, mid, lo = _split3(lf)
    c = _dot(tri, hi) + _dot(tri, mid) + _dot(tri, lo) + carry_ref[...]
    carry_ref[...] = c[tm - 1:tm, :]
    for h in range(nc_ref.shape[1] // HEAD_DIM):
        nc_ref[:, h * HEAD_DIM:(h + 1) * HEAD_DIM] = jnp.broadcast_to(c[:, h:h + 1] * (-LOG2E), (tm, HEAD_DIM))


def _fox_forget(x_bf, w_pad_bf, b_pad, seq):
    t, d = x_bf.shape
    tm = min(512, seq)
    return pl.pallas_call(
        functools.partial(_fox_forget_kernel, seq // tm),
        grid=(t // tm,),
        in_specs=[pl.BlockSpec((tm, d), lambda i: (i, 0)),
                  pl.BlockSpec((d, LANES), lambda i: (0, 0)),
                  pl.BlockSpec((1, LANES), lambda i: (0, 0))],
        out_specs=pl.BlockSpec((tm, FOX_F_COLS * HEAD_DIM), lambda i: (i, 0)),
        out_shape=jax.ShapeDtypeStruct((t, FOX_F_COLS * HEAD_DIM), F32),
        scratch_shapes=[pltpu.VMEM((1, LANES), F32)],
        compiler_params=_cparams(("arbitrary",)),
        name="fox_forget_cumsum",
    )(x_bf, w_pad_bf, b_pad)


def _bcast_row(a, period, row):
    n, w = a.shape
    a3 = a.reshape(n // period, period, w)
    return jnp.broadcast_to(a3[:, row:row + 1, :], a3.shape).reshape(n, w)


def _hgrn_kernel(q_ref, k_ref, v_ref, g_ref, hg_ref, gain_ref, tri_ref, u_ref, o_ref, st_ref):
    ts = q_ref.shape[0]
    n_chunks = ts // HG_CHUNK

    @pl.when(pl.program_id(2) == 0)
    def _():
        st_ref[...] = jnp.zeros_like(st_ref)

    q = q_ref[...].astype(F32)
    k = k_ref[...].astype(F32)
    v = v_ref[...]
    g = g_ref[...] * LOG2E
    g_hi = g.astype(BF16)
    g_lo = (g - g_hi.astype(F32)).astype(BF16)
    tri = tri_ref[...]
    b = _dot(tri, g_hi) + _dot(tri, g_lo)

    row128 = lax.broadcasted_iota(jnp.int32, (ts, HEAD_DIM), 0)
    row64 = lax.broadcasted_iota(jnp.int32, (ts, HG_CHUNK), 0) % HG_CHUNK
    col64 = lax.broadcasted_iota(jnp.int32, (ts, HG_CHUNK), 1)

    a_key = b - jnp.log2(k)
    parts = []
    for s in range(HG_SUB):
        a_s = _bcast_row(a_key, HG_SUB, s)
        parts.append((q * jnp.exp2(jnp.minimum(b - a_s, 0.0))).astype(BF16))
    scores = _dot(jnp.concatenate(parts, axis=1), u_ref[...])
    same_sub = (row64 // HG_SUB) == (col64 // HG_SUB)
    scores = jnp.where(same_sub & ((col64 % HG_SUB) <= (row64 % HG_SUB)), scores, 0.0)

    level_q, level_k, level_mask = [], [], []
    m = HG_SUB
    while m < HG_CHUNK:
        ref_b = _bcast_row(b, 2 * m, m - 1)
        second = (row128 % (2 * m)) >= m
        e = jnp.exp2(jnp.where(second, b - ref_b, ref_b - b))
        level_q.append((q * e).astype(BF16))
        level_k.append((k * e).astype(BF16))
        level_mask.append(((row64 // (2 * m)) == (col64 // (2 * m)))
                          & ((row64 % (2 * m)) >= m) & ((col64 % (2 * m)) < m))
        m *= 2

    b_last = _bcast_row(b, HG_CHUNK, HG_CHUNK - 1)
    q_in = (q * jnp.exp2(b)).astype(BF16)
    k_out = (k * jnp.exp2(b_last - b)).astype(BF16)

    chunks = [slice(c * HG_CHUNK, (c + 1) * HG_CHUNK) for c in range(n_chunks)]
    intra, state_in = [], []
    for sl in chunks:
        a = scores[sl]
        for lq, lk, lm in zip(level_q, level_k, level_mask):
            a = a + jnp.where(lm[sl], _dot_nt(lq[sl], lk[sl]), 0.0)
        intra.append(_dot(a.astype(BF16), v[sl]))
        state_in.append(_dot_tn(v[sl], k_out[sl]))

    gain = gain_ref[...]
    st = st_ref[...]
    for c, sl in enumerate(chunks):
        o = intra[c] + _dot_nt(q_in[sl], st.astype(BF16))
        st = st * jnp.exp2(b[sl.stop - 1:sl.stop, :]) + state_in[c]
        ms = jnp.mean(o * o, axis=-1, keepdims=True)
        o = o * lax.rsqrt(ms + NORM_EPS) * gain * hg_ref[sl, :].astype(F32)
        o_ref[sl, :] = o.astype(o_ref.dtype)
    st_ref[...] = st


def _hgrn(q, k, v, logf, hg_act, gain, batch, seq):
    d = q.shape[-1]
    heads = d // HEAD_DIM
    ts = min(HG_TS, seq)
    shp = (batch, seq, d)
    q, k, v, logf, hg_act = (a.reshape(shp) for a in (q, k, v, logf, hg_act))
    r = jnp.arange(ts)
    tri = ((r[:, None] // HG_CHUNK == r[None, :] // HG_CHUNK) & (r[None, :] <= r[:, None])).astype(BF16)
    u_rows = jnp.arange(HG_SUB * HEAD_DIM) // HEAD_DIM
    u = (u_rows[:, None] == (jnp.arange(HG_CHUNK)[None, :] % HG_SUB)).astype(BF16)
    tile = pl.BlockSpec((None, ts, HEAD_DIM), lambda b, h, s: (b, s, h))
    out = pl.pallas_call(
        _hgrn_kernel,
        grid=(batch, heads, seq // ts),
        in_specs=[tile, tile, tile, tile, tile,
                  pl.BlockSpec((1, HEAD_DIM), lambda b, h, s: (0, 0)),
                  pl.BlockSpec((ts, ts), lambda b, h, s: (0, 0)),
                  pl.BlockSpec((HG_SUB * HEAD_DIM, HG_CHUNK), lambda b, h, s: (0, 0))],
        out_specs=tile,
        out_shape=jax.ShapeDtypeStruct(shp, BF16),
        scratch_shapes=[pltpu.VMEM((HEAD_DIM, HEAD_DIM), F32)],
        compiler_params=_cparams(("parallel", "parallel", "arbitrary")),
        name="hgrn2",
    )(q, k, v, logf, hg_act, gain, tri, u)
    return out.reshape(batch * seq, d)


def _fox_kernel(qt_ref, k_ref, vt_ref, nc_ref, o_ref, acc_ref, sc_ref):
    tq = qt_ref.shape[1]
    tk = vt_ref.shape[2]
    n_streams = acc_ref.shape[0]
    qi = pl.program_id(2)
    cw = min(FOX_CW, tq)
    n_cg = tq // cw
    acc_ref[...] = jnp.zeros_like(acc_ref)
    ones_rows = jnp.where(lax.broadcasted_iota(jnp.int32, (FOX_VPAD, tk), 0) == 0, 1.0, 0.0).astype(BF16)

    chains = [(s, cg) for s in range(n_streams) for cg in range(n_cg)]

    def block_starts(g):
        return [pl.multiple_of((g * n_streams + s) * tk, tk) for s in range(n_streams)]

    def score_matmuls(g):
        starts = block_starts(g)
        for ci, (s, cg) in enumerate(chains):
            sc_ref[g % 2, ci] = _dot(k_ref[pl.ds(starts[s], tk), :], qt_ref[:, cg * cw:(cg + 1) * cw])

    def group(g, carry, masked):
        starts = block_starts(g)
        m_all, probs = [], []
        for ci, ((s, cg), m_old) in enumerate(zip(chains, carry)):
            if masked and s * tk >= (cg + 1) * cw:
                m_all.append(m_old)
                probs.append(None)
                continue
            st = sc_ref[g % 2, ci]
            st = st + jnp.concatenate([nc_ref[pl.ds(starts[s], tk), :]] * (cw // HEAD_DIM), axis=1)
            if masked:
                key_pos = starts[s] + lax.broadcasted_iota(jnp.int32, st.shape, 0)
                query_pos = qi * tq + cg * cw + lax.broadcasted_iota(jnp.int32, st.shape, 1)
                st = jnp.where(key_pos <= query_pos, st, FOX_MASKED)
            m_new = jnp.maximum(m_old, jnp.max(st, axis=0, keepdims=True))
            m_all.append(m_new)
            probs.append((jnp.exp2(m_old - m_new), jnp.exp2((st - m_new).astype(BF16))))
        if not masked:
            score_matmuls(g + 1)
        for (s, cg), ap in zip(chains, probs):
            if ap is not None:
                cols = slice(cg * cw, (cg + 1) * cw)
                vb = jnp.concatenate([vt_ref[g * n_streams + s], ones_rows], axis=0)
                acc_ref[s, :, cols] = acc_ref[s, :, cols] * ap[0] + _dot(vb, ap[1])
        return tuple(m_all)

    init = (jnp.full((1, cw), FOX_MASKED, F32),) * len(chains)

    score_matmuls(0)
    carry = lax.fori_loop(0, qi, lambda g, c: group(g, c, False), init)
    carry = group(qi, carry, True)
    carry = tuple(tuple(carry[s * n_cg + cg] for cg in range(n_cg)) for s in range(n_streams))

    for cg in range(n_cg):
        cols = slice(cg * cw, (cg + 1) * cw)
        m = carry[0][cg]
        for s in range(1, n_streams):
            m = jnp.maximum(m, carry[s][cg])
        acc = jnp.zeros((acc_ref.shape[1], cw), F32)
        for s in range(n_streams):
            acc = acc + jnp.exp2(carry[s][cg] - m) * acc_ref[s, :, cols]
        out = acc[:HEAD_DIM] * (1.0 / acc[HEAD_DIM:HEAD_DIM + 1])
        o_ref[cols, :] = out.T.astype(o_ref.dtype)


def _fox(q, k, v, nc_rep, batch, seq):
    d = q.shape[-1]
    heads = d // HEAD_DIM
    tq = min(FOX_TQ, seq)
    tk = min(FOX_TK, seq)
    shp = (batch, seq, d)
    qt = q.reshape(batch, seq, heads, HEAD_DIM).transpose(0, 2, 3, 1)
    vt = v.reshape(batch, seq // tk, tk, heads, HEAD_DIM).transpose(0, 3, 1, 4, 2)
    vrows = HEAD_DIM + FOX_VPAD
    kvfull = pl.BlockSpec((None, seq, HEAD_DIM), lambda b, h, i: (b, 0, h))
    out = pl.pallas_call(
        _fox_kernel,
        grid=(batch, heads, seq // tq),
        in_specs=[pl.BlockSpec((None, None, HEAD_DIM, tq), lambda b, h, i: (b, h, 0, i)),
                  kvfull,
                  pl.BlockSpec((None, None, seq // tk, HEAD_DIM, tk), lambda b, h, i: (b, h, 0, 0, 0)),
                  kvfull],
        out_specs=pl.BlockSpec((None, tq, HEAD_DIM), lambda b, h, i: (b, i, h)),
        out_shape=jax.ShapeDtypeStruct(shp, BF16),
        scratch_shapes=[pltpu.VMEM((tq // tk, vrows, tq), F32),
                        pltpu.VMEM((2, (tq // tk) * (tq // min(FOX_CW, tq)), tk, min(FOX_CW, tq)), F32)],
        compiler_params=_cparams(("parallel", "parallel", "arbitrary")),
        name="fox_attention",
    )(qt, k.reshape(shp), vt, nc_rep.reshape(shp))
    return out.reshape(batch * seq, d)


def _layer_norm(z, g, b):
    mu = jnp.mean(z, axis=-1, keepdims=True)
    zc = z - mu
    var = jnp.mean(zc * zc, axis=-1, keepdims=True)
    return zc * lax.rsqrt(var + LN_EPS) * g + b


def _first_index_of_max(vals, lane):
    vmax = jnp.max(vals, axis=-1, keepdims=True)
    idx = jnp.min(jnp.where(vals == vmax, lane, LANES), axis=-1, keepdims=True)
    return vmax, idx


def _merge_kernel(alpha, oa_ref, ob_ref, ga_ref, gb_ref, x_ref, wa_ref, wb_ref, wo_ref, g_ref, b_ref,
                  wr_hi_ref, wr_lo_ref, br_ref, x1_ref, gate_ref, idx_ref):
    y_a = _dot(oa_ref[...], wa_ref[...])
    y_b = _dot(ob_ref[...], wb_ref[...])
    merged = ga_ref[...].astype(F32) * y_a + gb_ref[...].astype(F32) * y_b
    h = _dot(merged.astype(BF16), wo_ref[...])
    x1 = _layer_norm(alpha * x_ref[...] + h, g_ref[...], b_ref[...])
    x1_ref[...] = x1

    x_hi = x1.astype(BF16)
    x_lo = (x1 - x_hi.astype(F32)).astype(BF16)
    wr_hi = wr_hi_ref[...]
    logits = _dot(x_hi, wr_hi) + _dot(x_lo, wr_hi) + _dot(x_hi, wr_lo_ref[...]) + br_ref[...]
    lane = lax.broadcasted_iota(jnp.int32, logits.shape, 1)
    neg = -jnp.inf
    g_logits = jnp.where(lane < N_GROUPS, logits, neg)
    g_max, grp = _first_index_of_max(g_logits, lane)
    p_grp = 1.0 / jnp.sum(jnp.exp(g_logits - g_max), axis=-1, keepdims=True)
    lo = N_GROUPS + EXPERTS_PER_GROUP * grp
    e_logits = jnp.where((lane >= lo) & (lane < lo + EXPERTS_PER_GROUP), logits, neg)
    v1, i1 = _first_index_of_max(e_logits, lane)
    v2, i2 = _first_index_of_max(jnp.where(lane == i1, neg, e_logits), lane)
    e2 = jnp.exp(v2 - v1)
    w1 = 1.0 / (1.0 + e2)
    gate_ref[...] = jnp.where(lane == 0, w1 * p_grp, jnp.where(lane == 1, e2 * w1 * p_grp, 0.0))
    idx_ref[...] = jnp.where(lane == 0, i1 - N_GROUPS, jnp.where(lane == 1, i2 - N_GROUPS, 0))


def _merge(alpha, o_a, o_b, ga, gb, x, wa, wb, wo, ln_g, ln_b, wr_hi, wr_lo, br):
    t, d = x.shape
    tm = min(MERGE_TM, t)
    row = lambda n: pl.BlockSpec((tm, n), lambda i: (i, 0))
    full = lambda a: pl.BlockSpec(a.shape, lambda i: (0, 0))
    return pl.pallas_call(
        functools.partial(_merge_kernel, alpha),
        grid=(t // tm,),
        in_specs=[row(d), row(d), row(d), row(d), row(d), full(wa), full(wb), full(wo), full(ln_g), full(ln_b),
                  full(wr_hi), full(wr_lo), full(br)],
        out_specs=[row(d), row(LANES), row(LANES)],
        out_shape=[jax.ShapeDtypeStruct((t, d), F32), jax.ShapeDtypeStruct((t, LANES), F32),
                   jax.ShapeDtypeStruct((t, LANES), jnp.int32)],
        compiler_params=_cparams(("parallel",)),
        name="merge_ln_router",
    )(o_a, o_b, ga, gb, x, wa, wb, wo, ln_g, ln_b, wr_hi, wr_lo, br)


def _moe_kernel(blk_e_ref, src_ref, src_next_ref, dst_ref, dst_prev_ref, x_hbm, wg_ref, wu_ref, wd_ref, y_hbm,
                xbuf, ybuf, wg_bf, wu_bf, wd_bf, gsem, ssem):
    i = pl.program_id(0)
    nb = pl.num_programs(0)
    blk = xbuf.shape[1]
    slot = i % 2
    other = 1 - slot

    def gather_row(idx_ref, s, r):
        return pltpu.make_async_copy(x_hbm.at[pl.ds(idx_ref[0, r], 1), :], xbuf.at[s, pl.ds(r, 1), :], gsem.at[s])

    def scatter_row(idx_ref, s, r):
        return pltpu.make_async_copy(ybuf.at[s, pl.ds(r, 1), :], y_hbm.at[pl.ds(idx_ref[0, r], 1), :], ssem.at[s])

    def for_rows(fn):
        def body(r, carry):
            fn(r)
            return carry
        lax.fori_loop(0, blk, body, 0, unroll=8)

    def wait_gather(s):
        pltpu.make_async_copy(x_hbm.at[pl.ds(0, blk), :], xbuf.at[s], gsem.at[s]).wait()

    def wait_scatter(s):
        pltpu.make_async_copy(ybuf.at[s], y_hbm.at[pl.ds(0, blk), :], ssem.at[s]).wait()

    @pl.when(i == 0)
    def _():
        for_rows(lambda r: gather_row(src_ref, 0, r).start())
        ybuf[1] = jnp.zeros((blk, ybuf.shape[2]), F32)

    @pl.when((i == 0) | (blk_e_ref[i] != blk_e_ref[jnp.maximum(i - 1, 0)]))
    def _():
        wg_bf[...] = wg_ref[...].astype(BF16)
        wu_bf[...] = wu_ref[...].astype(BF16)
        wd_bf[...] = wd_ref[...].astype(BF16)

    wait_gather(slot)

    @pl.when(i >= 1)
    def _():
        wait_scatter(slot)

    xb = xbuf[slot].astype(BF16)
    for r in range(blk):
        gather_row(src_next_ref, other, r).start()
        scatter_row(dst_prev_ref, other, r).start()
    gate = _dot(xb, wg_bf[...])
    up = _dot(xb, wu_bf[...])
    hidden = (gate * jax.nn.sigmoid(gate) * up).astype(BF16)
    ybuf[slot] = _dot(hidden, wd_bf[...])

    @pl.when(i == nb - 1)
    def _():
        for_rows(lambda r: scatter_row(dst_ref, slot, r).start())
        wait_scatter(other)
        wait_scatter(slot)
        wait_gather(other)


def _moe(x1, blk_e, src, dst, wg, wu, wd, layer):
    t, d = x1.shape
    n_blocks = blk_e.shape[0]
    blk = src.shape[-1]
    de = wg.shape[-1]
    smem_idx = lambda fn: pl.BlockSpec((None, 1, blk), fn, memory_space=pltpu.SMEM)
    cur = lambda i, be: (i, 0, 0)
    nxt = lambda i, be: (jnp.minimum(i + 1, n_blocks - 1), 0, 0)
    prv = lambda i, be: ((i + n_blocks - 1) % n_blocks, 0, 0)
    expert = lambda i, be: (layer, be[i], 0, 0)
    grid_spec = pltpu.PrefetchScalarGridSpec(
        num_scalar_prefetch=1,
        grid=(n_blocks,),
        in_specs=[smem_idx(cur), smem_idx(nxt), smem_idx(cur), smem_idx(prv),
                  pl.BlockSpec(memory_space=pl.ANY),
                  pl.BlockSpec((None, None, d, de), expert),
                  pl.BlockSpec((None, None, d, de), expert),
                  pl.BlockSpec((None, None, de, d), expert)],
        out_specs=pl.BlockSpec(memory_space=pl.ANY),
        scratch_shapes=[pltpu.VMEM((2, blk, d), F32), pltpu.VMEM((2, blk, d), F32),
                        pltpu.VMEM((d, de), BF16), pltpu.VMEM((d, de), BF16), pltpu.VMEM((de, d), BF16),
                        pltpu.SemaphoreType.DMA((2,)), pltpu.SemaphoreType.DMA((2,))],
    )
    return pl.pallas_call(
        _moe_kernel,
        grid_spec=grid_spec,
        out_shape=jax.ShapeDtypeStruct((n_blocks * blk, d), F32),
        compiler_params=_cparams(("arbitrary",)),
        name="moe_experts",
    )(blk_e, src, src, dst, dst, x1, wg, wu, wd)


def _moe_plan(expert_idx, t):
    a = t * TOP_K
    blk = min(MOE_BLK, a)
    n_blocks = a // blk + N_EXPERTS
    n_slots = n_blocks * blk
    flat_e = expert_idx.reshape(a)
    order = jnp.argsort(flat_e).astype(jnp.int32)
    counts = jnp.bincount(flat_e, length=N_EXPERTS).astype(jnp.int32)
    padded = (counts + blk - 1) // blk * blk
    pad_end = jnp.cumsum(padded)
    pad_start = pad_end - padded
    start = jnp.cumsum(counts) - counts
    n_used = (pad_end[-1] // blk).astype(jnp.int32)
    block_start = jnp.arange(n_blocks, dtype=jnp.int32) * blk
    blk_e = jnp.sum(block_start[:, None] >= pad_end[None, :], axis=1).astype(jnp.int32)
    blk_e = jnp.minimum(blk_e, N_EXPERTS - 1)
    blk_e = jnp.where(jnp.arange(n_blocks) < n_used, blk_e, blk_e[jnp.maximum(n_used - 1, 0)])
    slot = jnp.arange(n_slots, dtype=jnp.int32)
    slot_e = jnp.repeat(blk_e, blk)
    r = slot - pad_start[slot_e]
    valid = (r < counts[slot_e]) & (slot < pad_end[-1])
    assign = order[jnp.clip(start[slot_e] + r, 0, a - 1)]
    src = jnp.where(valid, assign // TOP_K, 0)
    pad_rank = jnp.cumsum(jnp.where(valid, 0, 1)) - 1
    dst = jnp.where(valid, (assign % TOP_K) * t + assign // TOP_K,
                    jnp.where(slot >= pad_end[-1], slot, a + pad_rank))
    return (blk_e, src.reshape(n_blocks, 1, blk).astype(jnp.int32),
            dst.reshape(n_blocks, 1, blk).astype(jnp.int32))


def _combine_kernel(alpha, x_ref, y0_ref, y1_ref, gate_ref, g_ref, b_ref, o_ref, obf_ref):
    gate = gate_ref[...]
    f = gate[:, 0:1] * y0_ref[...] + gate[:, 1:2] * y1_ref[...]
    out = _layer_norm(alpha * x_ref[...] + f, g_ref[...], b_ref[...])
    o_ref[...] = out
    obf_ref[...] = out.astype(BF16)


def _combine(alpha, x1, y, gate, ln_g, ln_b):
    t, d = x1.shape
    tm = min(COMB_TM, t)
    nt = t // tm
    row = lambda n: pl.BlockSpec((tm, n), lambda i: (i, 0))
    full = lambda a: pl.BlockSpec(a.shape, lambda i: (0, 0))
    return pl.pallas_call(
        functools.partial(_combine_kernel, alpha),
        grid=(nt,),
        in_specs=[row(d), row(d), pl.BlockSpec((tm, d), lambda i: (nt + i, 0)), row(LANES), full(ln_g), full(ln_b)],
        out_specs=[row(d), row(d)],
        out_shape=[jax.ShapeDtypeStruct((t, d), F32), jax.ShapeDtypeStruct((t, d), BF16)],
        compiler_params=_cparams(("parallel",)),
        name="combine_ln",
    )(x1, y, y, gate, ln_g, ln_b)


def kernel(x, w_in, b_fox_f, fox_q_gain, fox_k_gain, hg_lb_logits, hg_norm_gain, w_br_a, w_br_b, w_out, ln1_g, ln1_b, router_g_w, router_g_b, router_e_w, router_e_b, w_gate, w_up, w_down, ln2_g, ln2_b):
    batch, seq, d = x.shape
    depth = w_in.shape[0]
    t = batch * seq
    heads = d // HEAD_DIM
    alpha = (2 * depth) ** 0.25
    assert w_in.shape[2] == 9 * d + FOX_F_COLS

    lb_p = jax.nn.softmax(hg_lb_logits.astype(F32), axis=0)
    lb_all = jnp.cumsum(lb_p, axis=0) - lb_p[0]
    fox_scale = HEAD_DIM ** -0.5 * LOG2E

    xf = x.reshape(t, d)
    xb = xf.astype(BF16)
    for l in range(depth):
        w = w_in[l].astype(BF16)
        col = lambda j: w[:, j * d:(j + 1) * d]
        w_gates = w[:, 7 * d + FOX_F_COLS:]
        w_ff = jnp.pad(w[:, 7 * d:7 * d + FOX_F_COLS], ((0, 0), (0, LANES - FOX_F_COLS)))
        b_ff = jnp.pad(b_fox_f[l], (0, LANES - FOX_F_COLS)).reshape(1, LANES)

        (q_a,) = _proj(xb, col(0), "silu")
        logf, k_a = _proj(xb, col(1), "forget", aux=(lb_all[l].reshape(1, d),), out_dtypes=(F32, BF16))
        (v_a,) = _proj(xb, col(2), "plain")
        (hg_act,) = _proj(xb, col(3), "silu")
        (q_b,) = _proj(xb, col(4), "rms", aux=(jnp.tile(fox_q_gain[l] * fox_scale, heads).reshape(1, d),))
        (k_b,) = _proj(xb, col(5), "rms", aux=(jnp.tile(fox_k_gain[l], heads).reshape(1, d),))
        (v_b,) = _proj(xb, col(6), "plain")
        (g_a,) = _proj(xb, w_gates[:, :d], "sigmoid")
        (g_b,) = _proj(xb, w_gates[:, d:], "sigmoid")
        nc_rep = _fox_forget(xb, w_ff, b_ff, seq)

        o_a = _hgrn(q_a, k_a, v_a, logf, hg_act, hg_norm_gain[l].reshape(1, HEAD_DIM), batch, seq)

        o_b = _fox(q_b, k_b, v_b, nc_rep, batch, seq)

        w_r = jnp.pad(jnp.concatenate([router_g_w[l], router_e_w[l]], axis=1),
                      ((0, 0), (0, LANES - N_GROUPS - N_EXPERTS)))
        b_r = jnp.pad(jnp.concatenate([router_g_b[l], router_e_b[l]]), (0, LANES - N_GROUPS - N_EXPERTS)).reshape(1, LANES)
        w_r_hi = w_r.astype(BF16)
        w_r_lo = (w_r - w_r_hi.astype(F32)).astype(BF16)
        x1, gate, idx = _merge(alpha, o_a, o_b, g_a, g_b, xf, w_br_a[l].astype(BF16), w_br_b[l].astype(BF16),
                               w_out[l].astype(BF16), ln1_g[l].reshape(1, d), ln1_b[l].reshape(1, d),
                               w_r_hi, w_r_lo, b_r)

        blk_e, src, dst = _moe_plan(idx[:, :TOP_K], t)
        y = _moe(x1, blk_e, src, dst, w_gate, w_up, w_down, l)
        xf, xb = _combine(alpha, x1, y, gate, ln2_g[l].reshape(1, d), ln2_b[l].reshape(1, d))
    return xf.reshape(batch, seq, d)
```
